```python
import jax, jax.numpy as jnp
from jax import lax
import numpy as np

D_MODEL = 1024
BATCH = 4
SEQ = 4096
DEPTH = 4

N_MIXERS = 3
N_LAYERS_A = (DEPTH + 2) // 3
N_LAYERS_B = (DEPTH + 1) // 3
N_LAYERS_C = DEPTH // 3

EPS = 1e-6
NEG = -1e30
FORCE = 1e30
D_FF = 2816
FFN_RESID = 0.5
MOBA_HEADS = 16
MOBA_HEAD_DIM = D_MODEL // MOBA_HEADS
MOBA_BLOCK = 256
MOBA_TOPK = 3
MOBA_Q_CHUNK = 32
LRU_WIDTH = D_MODEL
LRU_BLOCKS = 4
LRU_BLOCK_W = LRU_WIDTH // LRU_BLOCKS
CONV_WIDTH = 4
LRU_C = 8.0
NSA_HEADS = 16
NSA_KV_GROUPS = 4
NSA_HEAD_DIM = 64
NSA_HPG = NSA_HEADS // NSA_KV_GROUPS
NSA_CMP_BLOCK = 32
NSA_CMP_STRIDE = 16
NSA_SEL_BLOCK = 64
NSA_TOPN = 16
NSA_LOCAL_BLOCKS = 2
NSA_WINDOW = 512
NSA_CMP_HIDDEN = 256
NSA_Q_CHUNK = 64
NSA_N_BRANCH = 3
NSA_KV_W = NSA_KV_GROUPS * NSA_HEAD_DIM
NSA_IN_WIDTH = NSA_HEADS * NSA_HEAD_DIM + 6 * NSA_KV_W + NSA_N_BRANCH * NSA_HEADS

kernel_name = "hybrid_moba_rglru_nsa_macaron"

F32 = jnp.float32


def rms_norm(x, g):
    xf = x.astype(F32)
    y = xf * lax.rsqrt(jnp.mean(xf * xf, axis=-1, keepdims=True) + EPS)
    return (y * g.astype(F32)).astype(x.dtype)


def swiglu_ffn(x, w_in, w_out):
    gate, up = jnp.split(x @ w_in, 2, axis=-1)
    return (jax.nn.silu(gate) * up) @ w_out


def moba_mixer(x, w_in, q_gain, k_gain, w_out):
    B, S, _ = x.shape
    H, hd, BS, Qc = MOBA_HEADS, MOBA_HEAD_DIM, MOBA_BLOCK, MOBA_Q_CHUNK
    nb = -(-S // BS)
    s_pad = nb * BS
    n_sel = min(MOBA_TOPK, nb)
    scale = hd ** -0.5
    q, k, v = jnp.split(x @ w_in, 3, axis=-1)
    q = rms_norm(q.reshape(B, S, H, hd), q_gain)
    k = rms_norm(k.reshape(B, S, H, hd), k_gain)
    v = v.reshape(B, S, H, hd)
    pad = ((0, 0), (0, s_pad - S), (0, 0), (0, 0))
    k_blk = jnp.pad(k, pad).reshape(B, nb, BS, H, hd).transpose(0, 3, 1, 2, 4)
    v_blk = jnp.pad(v, pad).reshape(B, nb, BS, H, hd).transpose(0, 3, 1, 2, 4)
    k_mean = jnp.mean(k_blk.astype(F32), axis=3).astype(k.dtype)
    n_chunks = S // Qc
    q_chunks = q.reshape(B, n_chunks, Qc, H, hd).transpose(1, 0, 3, 2, 4)
    b_ix = jnp.arange(B)[:, None, None, None]
    h_ix = jnp.arange(H)[None, :, None, None]
    blk_ids = jnp.arange(nb)

    def attend_chunk(args):
        ci, qc = args
        t0 = ci * Qc
        own = t0 // BS
        pos = t0 + jnp.arange(Qc)
        gate = jnp.einsum('bhqd,bhnd->bhqn', qc, k_mean, preferred_element_type=F32)
        gate = jnp.where(blk_ids < own, gate, NEG)
        _, sel = lax.top_k(gate, n_sel)
        sel_ok = sel < own
        k_sel = k_blk[b_ix, h_ix, sel]
        v_sel = v_blk[b_ix, h_ix, sel]
        k_own = lax.dynamic_index_in_dim(k_blk, own, axis=2, keepdims=False)
        v_own = lax.dynamic_index_in_dim(v_blk, own, axis=2, keepdims=False)
        s_own = jnp.einsum('bhqd,bhkd->bhqk', qc, k_own, preferred_element_type=F32) * scale
        own_mask = (own * BS + jnp.arange(BS))[None, :] <= pos[:, None]
        s_own = jnp.where(own_mask, s_own, NEG)
        s_sel = jnp.einsum('bhqd,bhqnkd->bhqnk', qc, k_sel, preferred_element_type=F32) * scale
        s_sel = jnp.where(sel_ok[..., None], s_sel, NEG).reshape(B, H, Qc, n_sel * BS)
        p = jax.nn.softmax(jnp.concatenate([s_own, s_sel], axis=-1), axis=-1).astype(v_blk.dtype)
        p_own = p[..., :BS]
        p_sel = p[..., BS:].reshape(B, H, Qc, n_sel, BS)
        return (jnp.einsum('bhqk,bhkd->bhqd', p_own, v_own)
                + jnp.einsum('bhqnk,bhqnkd->bhqd', p_sel, v_sel))

    o = lax.map(attend_chunk, (jnp.arange(n_chunks), q_chunks))
    o = o.transpose(1, 0, 3, 2, 4).reshape(B, S, H * hd)
    return o @ w_out


def _linear_recurrence(c1, c2):
    a1, b1 = c1
    a2, b2 = c2
    return a1 * a2, a2 * b1 + b2


def rglru_mixer(x, w_in, conv_w, conv_b, wa, ba, wx, bx, lam, w_out):
    B, S, _ = x.shape
    xb, yb = jnp.split(x @ w_in, 2, axis=-1)
    y = jax.nn.gelu(yb)
    xc = lax.conv_general_dilated(xb, conv_w, window_strides=(1,), padding=[(CONV_WIDTH - 1, 0)],
                                  dimension_numbers=('NWC', 'WIO', 'NWC'),
                                  feature_group_count=LRU_WIDTH) + conv_b
    xblk = xc.reshape(B, S, LRU_BLOCKS, LRU_BLOCK_W)
    r = jax.nn.sigmoid(jnp.einsum('bsnc,ncd->bsnd', xblk, wa).reshape(B, S, LRU_WIDTH) + ba).astype(F32)
    i = jax.nn.sigmoid(jnp.einsum('bsnc,ncd->bsnd', xblk, wx).reshape(B, S, LRU_WIDTH) + bx).astype(F32)
    log_a = -LRU_C * r * jax.nn.softplus(-lam.astype(F32))
    a = jnp.exp(log_a)
    mult = jnp.sqrt(-jnp.expm1(2.0 * log_a))
    mult = jnp.where((jnp.arange(S) == 0)[None, :, None], 1.0, mult)
    b = mult * i * xc.astype(F32)
    _, h = lax.associative_scan(_linear_recurrence, (a, b), axis=1)
    return (h.astype(x.dtype) * y) @ w_out


def nsa_mixer(x, w_in, gate_b, q_gain, kc_gain, ks_gain, kw_gain, pos_k, pos_v,
              ck_w1, ck_w2, cv_w1, cv_w2, w_out):
    B, S, _ = x.shape
    H, G, hpg, hd = NSA_HEADS, NSA_KV_GROUPS, NSA_HPG, NSA_HEAD_DIM
    l, d, ls, W, Qc = NSA_CMP_BLOCK, NSA_CMP_STRIDE, NSA_SEL_BLOCK, NSA_WINDOW, NSA_Q_CHUNK
    scale = hd ** -0.5
    n_cmp = (S - l) // d + 1
    n_slc = S // ls
    n_top = min(NSA_TOPN, n_slc)
    splits = [int(v) for v in np.cumsum([H * hd] + [NSA_KV_W] * 6)]
    q, kc, vc, ks, vs, kw, vw, g = jnp.split(x @ w_in, splits, axis=-1)
    q = rms_norm(q.reshape(B, S, G, hpg, hd), q_gain)
    gates = jax.nn.sigmoid(g + gate_b).reshape(B, S, G, hpg, NSA_N_BRANCH)
    cmp_idx = (jnp.arange(n_cmp) * d)[:, None] + jnp.arange(l)[None, :]

    def compress(t, pos_emb, w1, w2):
        blk = t.reshape(B, S, G, hd)[:, cmp_idx] + pos_emb[None, None, :, None, :]
        blk = blk.transpose(0, 1, 3, 2, 4).reshape(B, n_cmp, G, l * hd)
        return jax.nn.gelu(blk @ w1) @ w2

    k_cmp = rms_norm(compress(kc, pos_k, ck_w1, ck_w2), kc_gain)
    v_cmp = compress(vc, pos_v, cv_w1, cv_w2)
    cmp_end = jnp.arange(n_cmp) * d + l - 1
    c_start = jnp.arange(n_cmp) * d
    s_start = jnp.arange(n_slc) * ls
    overlap = ((c_start[:, None] < s_start[None, :] + ls)
               & (c_start[:, None] + l > s_start[None, :])).astype(F32)
    ks_blk = rms_norm(ks.reshape(B, S, G, hd), ks_gain).reshape(B, n_slc, ls, G, hd).transpose(0, 3, 1, 2, 4)
    vs_blk = vs.reshape(B, n_slc, ls, G, hd).transpose(0, 3, 1, 2, 4)
    wpad = ((0, 0), (W, 0), (0, 0), (0, 0))
    kw_pad = jnp.pad(rms_norm(kw.reshape(B, S, G, hd), kw_gain), wpad)
    vw_pad = jnp.pad(vw.reshape(B, S, G, hd), wpad)
    n_chunks = S // Qc
    q_chunks = q.reshape(B, n_chunks, Qc, G, hpg, hd).transpose(1, 0, 2, 3, 4, 5)
    g_chunks = gates.reshape(B, n_chunks, Qc, G, hpg, NSA_N_BRANCH).transpose(1, 0, 2, 3, 4, 5)
    b_ix = jnp.arange(B)[:, None, None, None]
    g_ix = jnp.arange(G)[None, :, None, None]
    blk_j = jnp.arange(n_slc)

    def attend_chunk(args):
        ci, qc, gc = args
        t0 = ci * Qc
        pos = t0 + jnp.arange(Qc)
        s_c = jnp.einsum('bqghd,bngd->bghqn', qc, k_cmp, preferred_element_type=F32) * scale
        m_c = cmp_end[None, :] <= pos[:, None]
        p_c = jnp.where(m_c, jax.nn.softmax(jnp.where(m_c, s_c, NEG), axis=-1), 0.0)
        o_c = jnp.einsum('bghqn,bngd->bqghd', p_c.astype(v_cmp.dtype), v_cmp)
        imp = jnp.einsum('bghqn,nj->bgqj', p_c, overlap)
        cur = (pos // ls)[:, None]
        forced = (blk_j[None, :] == 0) | ((cur - blk_j[None, :] >= 0) & (cur - blk_j[None, :] < NSA_LOCAL_BLOCKS))
        valid = (blk_j * ls)[None, :] <= pos[:, None]
        imp = jnp.where(valid, jnp.where(forced, FORCE, imp), NEG)
        _, sel = lax.top_k(imp, n_top)
        k_sel = ks_blk[b_ix, g_ix, sel]
        v_sel = vs_blk[b_ix, g_ix, sel]
        kpos = sel[..., None] * ls + jnp.arange(ls)
        m_s = (kpos <= pos[None, None, :, None, None])[:, :, None]
        s_s = jnp.einsum('bqghd,bgqnkd->bghqnk', qc, k_sel, preferred_element_type=F32) * scale
        s_s = jnp.where(m_s, s_s, NEG).reshape(B, G, hpg, Qc, n_top * ls)
        p_s = jax.nn.softmax(s_s, axis=-1).reshape(B, G, hpg, Qc, n_top, ls).astype(v_sel.dtype)
        o_s = jnp.einsum('bghqnk,bgqnkd->bqghd', p_s, v_sel)
        k_w = lax.dynamic_slice_in_dim(kw_pad, t0, W + Qc, axis=1)
        v_w = lax.dynamic_slice_in_dim(vw_pad, t0, W + Qc, axis=1)
        wpos = t0 - W + jnp.arange(W + Qc)
        m_w = ((wpos[None, :] <= pos[:, None]) & (wpos[None, :] > pos[:, None] - W) & (wpos[None, :] >= 0))
        s_w = jnp.einsum('bqghd,bkgd->bghqk', qc, k_w, preferred_element_type=F32) * scale
        p_w = jax.nn.softmax(jnp.where(m_w, s_w, NEG), axis=-1).astype(v_w.dtype)
        o_w = jnp.einsum('bghqk,bkgd->bqghd', p_w, v_w)
        return gc[..., 0:1] * o_c + gc[..., 1:2] * o_s + gc[..., 2:3] * o_w

    o = lax.map(attend_chunk, (jnp.arange(n_chunks), q_chunks, g_chunks))
    o = o.transpose(1, 0, 2, 3, 4, 5).reshape(B, S, H * hd)
    return o @ w_out


def _normal(k, shape, scale):
    return jax.random.normal(k, shape, F32) * scale


def setup_inputs(seed: int = 0) -> dict:
    key = jax.random.key(seed)
    ks = iter(jax.random.split(key, 40))
    D, F = D_MODEL, D_FF
    nA, nB, nC = N_LAYERS_A, N_LAYERS_B, N_LAYERS_C
    Wd, hdA, hdC = LRU_WIDTH, MOBA_HEAD_DIM, NSA_HEAD_DIM
    a0 = jax.random.uniform(next(ks), (nB, Wd), F32, minval=0.9, maxval=0.999)
    s = a0 ** (1.0 / LRU_C)
    return {
        "x": _normal(next(ks), (BATCH, SEQ, D), 1.0),
        "norm_g": 1.0 + _normal(next(ks), (DEPTH, 3, D), 0.01),
        "ffn1_wi": _normal(next(ks), (DEPTH, D, 2 * F), D ** -0.5),
        "ffn1_wo": _normal(next(ks), (DEPTH, F, D), F ** -0.5),
        "ffn2_wi": _normal(next(ks), (DEPTH, D, 2 * F), D ** -0.5),
        "ffn2_wo": _normal(next(ks), (DEPTH, F, D), F ** -0.5),
        "moba_w_in": _normal(next(ks), (nA, D, 3 * MOBA_HEADS * hdA), D ** -0.5),
        "moba_q_gain": 1.0 + _normal(next(ks), (nA, hdA), 0.01),
        "moba_k_gain": 1.0 + _normal(next(ks), (nA, hdA), 0.01),
        "moba_w_out": _normal(next(ks), (nA, MOBA_HEADS * hdA, D), (MOBA_HEADS * hdA) ** -0.5),
        "lru_w_in": _normal(next(ks), (nB, D, 2 * Wd), D ** -0.5),
        "lru_conv_w": _normal(next(ks), (nB, CONV_WIDTH, 1, Wd), CONV_WIDTH ** -0.5),
        "lru_conv_b": _normal(next(ks), (nB, Wd), 0.01),
        "lru_wa": _normal(next(ks), (nB, LRU_BLOCKS, LRU_BLOCK_W, LRU_BLOCK_W), LRU_BLOCK_W ** -0.5),
        "lru_ba": _normal(next(ks), (nB, Wd), 0.01),
        "lru_wx": _normal(next(ks), (nB, LRU_BLOCKS, LRU_BLOCK_W, LRU_BLOCK_W), LRU_BLOCK_W ** -0.5),
        "lru_bx": _normal(next(ks), (nB, Wd), 0.01),
        "lru_lam": jnp.log(s) - jnp.log1p(-s),
        "lru_w_out": _normal(next(ks), (nB, Wd, D), Wd ** -0.5),
        "nsa_w_in": _normal(next(ks), (nC, D, NSA_IN_WIDTH), D ** -0.5),
        "nsa_gate_b": _normal(next(ks), (nC, NSA_N_BRANCH * NSA_HEADS), 0.01),
        "nsa_q_gain": 1.0 + _normal(next(ks), (nC, hdC), 0.01),
        "nsa_kc_gain": 1.0 + _normal(next(ks), (nC, hdC), 0.01),
        "nsa_ks_gain": 1.0 + _normal(next(ks), (nC, hdC), 0.01),
        "nsa_kw_gain": 1.0 + _normal(next(ks), (nC, hdC), 0.01),
        "nsa_pos_k": _normal(next(ks), (nC, NSA_CMP_BLOCK, hdC), 0.1),
        "nsa_pos_v": _normal(next(ks), (nC, NSA_CMP_BLOCK, hdC), 0.1),
        "nsa_ck_w1": _normal(next(ks), (nC, NSA_CMP_BLOCK * hdC, NSA_CMP_HIDDEN), (NSA_CMP_BLOCK * hdC) ** -0.5),
        "nsa_ck_w2": _normal(next(ks), (nC, NSA_CMP_HIDDEN, hdC), NSA_CMP_HIDDEN ** -0.5),
        "nsa_cv_w1": _normal(next(ks), (nC, NSA_CMP_BLOCK * hdC, NSA_CMP_HIDDEN), (NSA_CMP_BLOCK * hdC) ** -0.5),
        "nsa_cv_w2": _normal(next(ks), (nC, NSA_CMP_HIDDEN, hdC), NSA_CMP_HIDDEN ** -0.5),
        "nsa_w_out": _normal(next(ks), (nC, NSA_HEADS * hdC, D), (NSA_HEADS * hdC) ** -0.5),
    }


def reference(x, norm_g, ffn1_wi, ffn1_wo, ffn2_wi, ffn2_wo,
              moba_w_in, moba_q_gain, moba_k_gain, moba_w_out,
              lru_w_in, lru_conv_w, lru_conv_b, lru_wa, lru_ba, lru_wx, lru_bx, lru_lam, lru_w_out,
              nsa_w_in, nsa_gate_b, nsa_q_gain, nsa_kc_gain, nsa_ks_gain, nsa_kw_gain,
              nsa_pos_k, nsa_pos_v, nsa_ck_w1, nsa_ck_w2, nsa_cv_w1, nsa_cv_w2, nsa_w_out):
    h = x
    for i in range(DEPTH):
        j = i // N_MIXERS
        kind = i % N_MIXERS
        h = h + FFN_RESID * swiglu_ffn(rms_norm(h, norm_g[i, 0]), ffn1_wi[i], ffn1_wo[i])
        u = rms_norm(h, norm_g[i, 1])
        if kind == 0:
            m = moba_mixer(u, moba_w_in[j], moba_q_gain[j], moba_k_gain[j], moba_w_out[j])
        elif kind == 1:
            m = rglru_mixer(u, lru_w_in[j], lru_conv_w[j], lru_conv_b[j], lru_wa[j], lru_ba[j],
                            lru_wx[j], lru_bx[j], lru_lam[j], lru_w_out[j])
        else:
            m = nsa_mixer(u, nsa_w_in[j], nsa_gate_b[j], nsa_q_gain[j], nsa_kc_gain[j], nsa_ks_gain[j],
                          nsa_kw_gain[j], nsa_pos_k[j], nsa_pos_v[j], nsa_ck_w1[j], nsa_ck_w2[j],
                          nsa_cv_w1[j], nsa_cv_w2[j], nsa_w_out[j])
        h = h + m
        h = h + FFN_RESID * swiglu_ffn(rms_norm(h, norm_g[i, 2]), ffn2_wi[i], ffn2_wo[i])
    return h
```

```python
import functools

import jax
import jax.numpy as jnp
from jax import lax
from jax.experimental import pallas as pl
from jax.experimental.pallas import tpu as pltpu

F32 = jnp.float32
BF16 = jnp.bfloat16

EPS = 1e-6
NEG = -1e30
FORCE = 1e30
FFN_RESID = 0.5

MOBA_HEADS = 16
MOBA_BLOCK = 256
MOBA_TOPK = 3

LRU_BLOCKS = 4
CONV_WIDTH = 4
LRU_C = 8.0

NSA_HEADS = 16
NSA_KV_GROUPS = 4
NSA_HEAD_DIM = 64
NSA_HPG = NSA_HEADS // NSA_KV_GROUPS
NSA_CMP_BLOCK = 32
NSA_CMP_STRIDE = 16
NSA_SEL_BLOCK = 64
NSA_TOPN = 16
NSA_LOCAL_BLOCKS = 2
NSA_WINDOW = 512
NSA_N_BRANCH = 3

V7X_VMEM_BYTES = 64 * 1024 * 1024
VMEM_LIMIT_BYTES = V7X_VMEM_BYTES - 8 * 1024 * 1024

ATTN_TILE = 256


def _params(*sem):
    return pltpu.CompilerParams(dimension_semantics=sem, vmem_limit_bytes=VMEM_LIMIT_BYTES)


def _resident(shape):
    nd = len(shape)
    return pl.BlockSpec(shape, lambda *_: (0,) * nd, pipeline_mode=pl.Buffered(1))


def _rms_rows(x, g):
    return x * lax.rsqrt(jnp.mean(x * x, axis=-1, keepdims=True) + EPS) * g


def _gelu_tanh(x):
    c = 0.7978845608028654
    return 0.5 * x * (1.0 + jnp.tanh(c * (x + 0.044715 * (x * x * x))))


def _dot(a, b):
    return jnp.dot(a, b, preferred_element_type=F32)


def _dot_nt(a, b):
    return lax.dot_general(a, b, (((1,), (1,)), ((), ())), preferred_element_type=F32)


def _ffn_kernel(x_ref, g_ref, wi_ref, wo_ref, o_ref, *, d_ff, chunk):
    x = x_ref[...]
    xn = _rms_rows(x, g_ref[...]).astype(BF16)
    acc = jnp.zeros(x.shape, F32)
    for c0 in range(0, d_ff, chunk):
        gate = _dot(xn, wi_ref[:, c0:c0 + chunk])
        up = _dot(xn, wi_ref[:, d_ff + c0:d_ff + c0 + chunk])
        act = (gate * jax.nn.sigmoid(gate) * up).astype(BF16)
        acc = acc + _dot(act, wo_ref[c0:c0 + chunk, :])
    o_ref[...] = x + FFN_RESID * acc


def _ffn(h, g, wi, wo, *, tm=512, chunk=256):
    n, d = h.shape
    d_ff = wo.shape[0]
    return pl.pallas_call(
        functools.partial(_ffn_kernel, d_ff=d_ff, chunk=chunk),
        grid=(n // tm,),
        in_specs=[
            pl.BlockSpec((tm, d), lambda i: (i, 0)),
            _resident((1, d)),
            _resident(wi.shape),
            _resident(wo.shape),
        ],
        out_specs=pl.BlockSpec((tm, d), lambda i: (i, 0)),
        out_shape=jax.ShapeDtypeStruct((n, d), F32),
        compiler_params=_params("parallel"),
        name="ffn",
    )(h, g.reshape(1, d), wi, wo)


def _proj_t_kernel(x_ref, g_ref, wt_ref, col_ref, o_ref, *aux_refs, segs, head_dim):
    xn = _rms_rows(x_ref[...], g_ref[...]).astype(BF16)
    tm = xn.shape[0]
    for r0, r1, mode in segs:
        acc = _dot_nt(wt_ref[r0:r1, :], xn)
        if mode == "raw":
            o_ref[r0:r1, :] = acc.astype(o_ref.dtype)
        elif mode == "norm":
            nh = (r1 - r0) // head_dim
            a3 = acc.reshape(nh, head_dim, tm)
            ms = jnp.mean(a3 * a3, axis=1, keepdims=True)
            y = a3 * lax.rsqrt(ms + EPS) * col_ref[r0:r1, :].reshape(nh, head_dim, 1)
            o_ref[r0:r1, :] = y.reshape(r1 - r0, tm).astype(o_ref.dtype)
        else:
            (aux_ref,) = aux_refs
            aux_ref[...] = jax.nn.sigmoid(acc + col_ref[r0:r1, :])


def _proj_t(h, g, wt, col, segs, *, head_dim, tm=512):
    n, d = h.shape
    rows = wt.shape[0]
    n_main = max(r1 for _, r1, mode in segs if mode != "sigmoid")
    aux = [(r0, r1) for r0, r1, mode in segs if mode == "sigmoid"]
    out_shape = [jax.ShapeDtypeStruct((n_main, n), BF16)]
    out_specs = [pl.BlockSpec((n_main, tm), lambda i: (0, i))]
    if aux:
        ((a0, a1),) = aux
        out_shape.append(jax.ShapeDtypeStruct((a1 - a0, n), F32))
        out_specs.append(pl.BlockSpec((a1 - a0, tm), lambda i: (0, i)))
    return pl.pallas_call(
        functools.partial(_proj_t_kernel, segs=segs, head_dim=head_dim),
        grid=(n // tm,),
        in_specs=[
            pl.BlockSpec((tm, d), lambda i: (i, 0)),
            _resident((1, d)),
            _resident((rows, d)),
            _resident((rows, 1)),
        ],
        out_specs=out_specs,
        out_shape=out_shape,
        compiler_params=_params("parallel"),
        name="proj_t",
    )(h, g.reshape(1, d), wt, col)


def _out_proj_kernel(h_ref, a_ref, w_ref, o_ref):
    o_ref[...] = h_ref[...] + _dot(a_ref[...], w_ref[...])


def _out_proj(h, a, w, *, tm=1024):
    n, d = h.shape
    k = a.shape[1]
    return pl.pallas_call(
        _out_proj_kernel,
        grid=(n // tm,),
        in_specs=[
            pl.BlockSpec((tm, d), lambda i: (i, 0)),
            pl.BlockSpec((tm, k), lambda i: (i, 0)),
            _resident((k, d)),
        ],
        out_specs=pl.BlockSpec((tm, d), lambda i: (i, 0)),
        out_shape=jax.ShapeDtypeStruct((n, d), F32),
        compiler_params=_params("parallel"),
        name="out_proj",
    )(h, a, w)


def _softmax_tile_init(s, v_t):
    m = jnp.max(s, axis=0, keepdims=True)
    p = jnp.exp(s - m)
    l = jnp.sum(p, axis=0, keepdims=True)
    return m, l, _dot(v_t, p.astype(BF16))


def _softmax_tile_update(carry, s, v_t):
    m, l, acc = carry
    m_new = jnp.maximum(m, jnp.max(s, axis=0, keepdims=True))
    alpha = jnp.exp(m - m_new)
    p = jnp.exp(s - m_new)
    l = alpha * l + jnp.sum(p, axis=0, keepdims=True)
    return m_new, l, alpha * acc + _dot(v_t, p.astype(BF16))


def _moba_kernel(qt_ref, kn_ref, vt_ref, ot_ref, bias_ref, *, seq):
    t = ATTN_TILE
    nb = seq // t
    avg = jnp.where(lax.broadcasted_iota(jnp.int32, (nb, seq), 1) // t
                    == lax.broadcasted_iota(jnp.int32, (nb, seq), 0), 1.0 / t, 0.0).astype(BF16)
    kmean = _dot(avg, kn_ref[...])
    km_hi = kmean.astype(BF16)
    km_lo = (kmean - km_hi.astype(F32)).astype(BF16)
    blk = lax.broadcasted_iota(jnp.int32, (nb, t), 0)
    causal = (lax.broadcasted_iota(jnp.int32, (t, t), 0) <= lax.broadcasted_iota(jnp.int32, (t, t), 1))

    def q_tile(qi, _):
        q0 = pl.multiple_of(qi * t, t)
        q = qt_ref[:, pl.ds(q0, t)]
        past = blk < qi
        gate = jnp.where(past, _dot(km_hi, q) + _dot(km_lo, q), NEG)
        sel = jnp.zeros((nb, t), jnp.bool_)
        for _ in range(min(MOBA_TOPK, nb)):
            top = jnp.max(gate, axis=0, keepdims=True)
            first = jnp.min(jnp.where(gate == top, blk, nb), axis=0, keepdims=True)
            hit = blk == first
            sel = jnp.logical_or(sel, hit)
            gate = jnp.where(hit, -jnp.inf, gate)
        bias_ref[...] = jnp.where(jnp.logical_and(sel, past), 0.0, NEG)

        s = jnp.where(causal, _dot(kn_ref[pl.ds(q0, t), :], q), NEG)
        carry = _softmax_tile_init(s, vt_ref[:, pl.ds(q0, t)])

        def kv_tile(j, carry):
            k0 = pl.multiple_of(j * t, t)
            s = _dot(kn_ref[pl.ds(k0, t), :], q) + bias_ref[pl.ds(j, 1), :]
            return _softmax_tile_update(carry, s, vt_ref[:, pl.ds(k0, t)])

        _, l, acc = lax.fori_loop(0, qi, kv_tile, carry)
        ot_ref[:, pl.ds(q0, t)] = (acc / l).astype(ot_ref.dtype)
        return 0

    lax.fori_loop(0, nb, q_tile, 0)


def _moba_attention(qkv_t, kn, *, batch, seq, heads, head_dim):
    n = batch * seq
    return pl.pallas_call(
        functools.partial(_moba_kernel, seq=seq),
        grid=(batch, heads),
        in_specs=[
            pl.BlockSpec((head_dim, seq), lambda b, h: (h, b)),
            pl.BlockSpec((None, None, seq, head_dim), lambda b, h: (b, h, 0, 0)),
            pl.BlockSpec((head_dim, seq), lambda b, h: (2 * heads + h, b)),
        ],
        out_specs=pl.BlockSpec((head_dim, seq), lambda b, h: (h, b)),
        out_shape=jax.ShapeDtypeStruct((heads * head_dim, n), BF16),
        scratch_shapes=[pltpu.VMEM((seq // ATTN_TILE, ATTN_TILE), F32)],
        compiler_params=_params("parallel", "parallel"),
        name="moba_attention",
    )(qkv_t, kn, qkv_t)


def _moba_layer(h, g, w_in, q_gain, k_gain, w_out, *, batch, seq):
    heads = MOBA_HEADS
    d = h.shape[1]
    hd = w_in.shape[1] // (3 * heads)
    hw = heads * hd
    assert seq % MOBA_BLOCK == 0 and MOBA_BLOCK == ATTN_TILE
    scale = hd ** -0.5
    col = jnp.concatenate([jnp.tile(q_gain, heads) * scale, jnp.tile(k_gain, heads), jnp.ones((hw,), F32)])
    segs = ((0, hw, "norm"), (hw, 2 * hw, "norm"), (2 * hw, 3 * hw, "raw"))
    (qkv_t,) = _proj_t(h, g, w_in.T.astype(BF16), col.reshape(-1, 1), segs, head_dim=hd)
    kn = qkv_t[hw:2 * hw].reshape(heads, hd, batch, seq).transpose(2, 0, 3, 1)
    o_t = _moba_attention(qkv_t, kn, batch=batch, seq=seq, heads=heads, head_dim=hd)
    return _out_proj(h, o_t.T, w_out.astype(BF16))


def _lru_proj_kernel(x_ref, g_ref, w_ref, xb_ref, y_ref, *, width):
    xn = _rms_rows(x_ref[...], g_ref[...]).astype(BF16)
    xb_ref[...] = _dot(xn, w_ref[:, :width])
    y_ref[...] = _gelu_tanh(_dot(xn, w_ref[:, width:]))


def _lru_proj(h, g, w, *, tm=512):
    n, d = h.shape
    width = w.shape[1] // 2
    return pl.pallas_call(
        functools.partial(_lru_proj_kernel, width=width),
        grid=(n // tm,),
        in_specs=[pl.BlockSpec((tm, d), lambda i: (i, 0)), _resident((1, d)), _resident(w.shape)],
        out_specs=[pl.BlockSpec((tm, width), lambda i: (i, 0))] * 2,
        out_shape=[jax.ShapeDtypeStruct((n, width), F32)] * 2,
        compiler_params=_params("parallel"),
        name="lru_proj",
    )(h, g.reshape(1, d), w)


LRU_CONV_PAD = 8


def _lru_kernel(xb_ref, y_ref, cw_ref, cb_ref, wa_ref, ba_ref, wx_ref, bx_ref, lam_ref, o_ref,
                xpad, a_s, b_s, h_s, h_carry, *, tile, width):
    si = pl.program_id(1)
    pad = LRU_CONV_PAD

    @pl.when(si == 0)
    def _():
        xpad[0:pad, :] = jnp.zeros((pad, width), F32)
        h_carry[...] = jnp.zeros((1, width), F32)

    xpad[pad:pad + tile, :] = xb_ref[...]
    xc = cb_ref[...] + cw_ref[0:1, :] * xpad[pad - CONV_WIDTH + 1:pad - CONV_WIDTH + 1 + tile, :]
    for k in range(1, CONV_WIDTH):
        off = pad - CONV_WIDTH + 1 + k
        xc = xc + cw_ref[k:k + 1, :] * xpad[off:off + tile, :]
    xcb = xc.astype(BF16)
    first = (lax.broadcasted_iota(jnp.int32, (tile, 1), 0) + si * tile) == 0
    bw = width // LRU_BLOCKS
    for n in range(LRU_BLOCKS):
        sl = slice(n * bw, (n + 1) * bw)
        r = jax.nn.sigmoid(_dot(xcb[:, sl], wa_ref[n]) + ba_ref[:, sl])
        i = jax.nn.sigmoid(_dot(xcb[:, sl], wx_ref[n]) + bx_ref[:, sl])
        z = -lam_ref[:, sl]
        softplus = jnp.maximum(z, 0.0) + jnp.log1p(jnp.exp(-jnp.abs(z)))
        a = jnp.exp(-LRU_C * r * softplus)
        mult = jnp.where(first, 1.0, jnp.sqrt(1.0 - a * a))
        a_s[:, sl] = a
        b_s[:, sl] = mult * i * xc[:, sl]

    def step(t, h):
        h = a_s[pl.ds(t, 1), :] * h + b_s[pl.ds(t, 1), :]
        h_s[pl.ds(t, 1), :] = h
        return h

    h_carry[...] = lax.fori_loop(0, tile, step, h_carry[...], unroll=8)
    o_ref[...] = (h_s[...] * y_ref[...]).astype(o_ref.dtype)
    xpad[0:pad, :] = xpad[tile:tile + pad, :]


def _lru_recurrence(xb, y, cw, cb, wa, ba, wx, bx, lam, *, batch, seq, tile=512):
    n, width = xb.shape
    nt = seq // tile
    row = lambda b, s: (b * nt + s, 0)
    vec = _resident((1, width))
    return pl.pallas_call(
        functools.partial(_lru_kernel, tile=tile, width=width),
        grid=(batch, nt),
        in_specs=[
            pl.BlockSpec((tile, width), row),
            pl.BlockSpec((tile, width), row),
            _resident(cw.shape), vec,
            _resident(wa.shape), vec,
            _resident(wx.shape), vec,
            vec,
        ],
        out_specs=pl.BlockSpec((tile, width), row),
        out_shape=jax.ShapeDtypeStruct((n, width), BF16),
        scratch_shapes=[
            pltpu.VMEM((tile + LRU_CONV_PAD, width), F32),
            pltpu.VMEM((tile, width), F32),
            pltpu.VMEM((tile, width), F32),
            pltpu.VMEM((tile, width), F32),
            pltpu.VMEM((1, width), F32),
        ],
        compiler_params=_params("parallel", "arbitrary"),
        name="lru_recurrence",
    )(xb, y, cw, cb.reshape(1, width), wa, ba.reshape(1, width), wx, bx.reshape(1, width), lam.reshape(1, width))


def _lru_layer(h, g, w_in, conv_w, conv_b, wa, ba, wx, bx, lam, w_out, *, batch, seq):
    xb, y = _lru_proj(h, g, w_in.astype(BF16))
    gated = _lru_recurrence(xb, y, conv_w.reshape(CONV_WIDTH, -1), conv_b, wa.astype(BF16), ba, wx.astype(BF16), bx, lam,
                            batch=batch, seq=seq)
    return _out_proj(h, gated, w_out.astype(BF16))


def _nsa_compress_kernel(rk_ref, rv_ref, pk_ref, pv_ref, kw1_ref, kw2_ref, vw1_ref, vw2_ref, kg_ref, ko_ref, vo_ref):
    def compress(r_ref, pos_ref, w1_ref, w2_ref):
        r = r_ref[...]
        half = r.shape[1]
        rows = r.shape[0]
        lo = _dot(r, w1_ref[:half, :])
        hi = _dot(r, w1_ref[half:, :])
        hi_next = pltpu.roll(hi, shift=rows - 1, axis=0)
        pos = _dot(pos_ref[...], w1_ref[...])[0:1, :]
        hidden = _gelu_tanh(lo + hi_next + pos)
        return _dot(hidden.astype(BF16), w2_ref[...])

    ko_ref[...] = _rms_rows(compress(rk_ref, pk_ref, kw1_ref, kw2_ref), kg_ref[...]).astype(ko_ref.dtype)
    vo_ref[...] = compress(rv_ref, pv_ref, vw1_ref, vw2_ref).astype(vo_ref.dtype)


def _nsa_compress(rk, rv, pos_k, pos_v, kw1, kw2, vw1, vw2, k_gain):
    bg, rows, wide = rk.shape
    hd = kw2.shape[1]
    blk = pl.BlockSpec((None, rows, wide), lambda i: (i, 0, 0))
    out = pl.BlockSpec((None, rows, hd), lambda i: (i, 0, 0))
    return pl.pallas_call(
        _nsa_compress_kernel,
        grid=(bg,),
        in_specs=[blk, blk, _resident(pos_k.shape), _resident(pos_v.shape), _resident(kw1.shape), _resident(kw2.shape),
                  _resident(vw1.shape), _resident(vw2.shape), _resident((1, hd))],
        out_specs=[out, out],
        out_shape=[jax.ShapeDtypeStruct((bg, rows, hd), BF16)] * 2,
        compiler_params=_params("parallel"),
        name="nsa_compress",
    )(rk, rv, pos_k, pos_v, kw1, kw2, vw1, vw2, k_gain.reshape(1, hd))


def _nsa_kernel(qt_ref, kc_ref, vct_ref, ks_ref, vst_ref, kw_ref, vwt_ref, gt_ref, ot_ref, *, seq):
    t = ATTN_TILE
    hd = NSA_HEAD_DIM
    ncr = seq // NSA_CMP_STRIDE
    nsl = seq // NSA_SEL_BLOCK
    n_top = min(NSA_TOPN, nsl)
    jn = lax.broadcasted_iota(jnp.int32, (nsl, ncr), 0) * NSA_SEL_BLOCK
    cn = lax.broadcasted_iota(jnp.int32, (nsl, ncr), 1) * NSA_CMP_STRIDE
    ov_t = jnp.where(jnp.logical_and(cn < jn + NSA_SEL_BLOCK, cn + NSA_CMP_BLOCK > jn), 1.0, 0.0).astype(BF16)
    kv_off = lax.broadcasted_iota(jnp.int32, (t, t), 0)
    q_off = lax.broadcasted_iota(jnp.int32, (t, t), 1)
    causal = kv_off <= q_off
    cmp_end = lax.broadcasted_iota(jnp.int32, (ncr, t), 0) * NSA_CMP_STRIDE + (NSA_CMP_BLOCK - 1)
    blk_j = lax.broadcasted_iota(jnp.int32, (nsl, t), 0)

    def q_tile(qi, _):
        q0 = pl.multiple_of(qi * t, t)
        qs = [qt_ref[hp * hd:(hp + 1) * hd, pl.ds(q0, t)] for hp in range(NSA_HPG)]

        m_c = cmp_end <= q0 + lax.broadcasted_iota(jnp.int32, (ncr, t), 1)
        p_sum = jnp.zeros((ncr, t), F32)
        o_cmp = []
        for q in qs:
            s = jnp.where(m_c, _dot(kc_ref[...], q), NEG)
            e = jnp.where(m_c, jnp.exp(s - jnp.max(s, axis=0, keepdims=True)), 0.0)
            den = jnp.sum(e, axis=0, keepdims=True)
            p = e / jnp.where(den > 0.0, den, 1.0)
            o_cmp.append(_dot(vct_ref[...], p.astype(BF16)))
            p_sum = p_sum + p
        ps_hi = p_sum.astype(BF16)
        ps_lo = (p_sum - ps_hi.astype(F32)).astype(BF16)
        imp = _dot(ov_t, ps_hi) + _dot(ov_t, ps_lo)
        qpos = q0 + lax.broadcasted_iota(jnp.int32, (nsl, t), 1)
        back = qpos // NSA_SEL_BLOCK - blk_j
        forced = jnp.logical_or(blk_j == 0, jnp.logical_and(back >= 0, back < NSA_LOCAL_BLOCKS))
        valid = blk_j * NSA_SEL_BLOCK <= qpos
        score = jnp.where(valid, jnp.where(forced, FORCE, imp), NEG)
        rank = jnp.zeros((nsl, t), jnp.int32)
        for m in range(nsl):
            row = score[m:m + 1, :]
            rank = rank + jnp.where(blk_j > m, jnp.where(row >= score, 1, 0), jnp.where(row > score, 1, 0))
        sel = jnp.logical_and(rank < n_top, valid)
        bias = jnp.where(sel, 0.0, NEG).astype(BF16)

        for hp, q in enumerate(qs):
            q_aug = jnp.concatenate([q, bias], axis=0)
            s = jnp.where(causal, _dot(ks_ref[pl.ds(q0, t), :], q_aug), NEG)
            carry = _softmax_tile_init(s, vst_ref[:, pl.ds(q0, t)])

            def sel_tile(j, carry, q_aug=q_aug):
                k0 = pl.multiple_of(j * t, t)
                return _softmax_tile_update(carry, _dot(ks_ref[pl.ds(k0, t), :], q_aug), vst_ref[:, pl.ds(k0, t)])

            _, l, acc = lax.fori_loop(0, qi, sel_tile, carry)
            o_sel = acc / l

            s = jnp.where(causal, _dot(kw_ref[pl.ds(q0, t), :], q), NEG)
            carry = _softmax_tile_init(s, vwt_ref[:, pl.ds(q0, t)])

            def win_prev(carry, q=q):
                k0 = pl.multiple_of((qi - 1) * t, t)
                return _softmax_tile_update(carry, _dot(kw_ref[pl.ds(k0, t), :], q), vwt_ref[:, pl.ds(k0, t)])

            def win_tail(carry, q=q):
                k0 = pl.multiple_of((qi - 2) * t, t)
                s = jnp.where(kv_off > q_off, _dot(kw_ref[pl.ds(k0, t), :], q), NEG)
                return _softmax_tile_update(carry, s, vwt_ref[:, pl.ds(k0, t)])

            carry = lax.cond(qi >= 1, win_prev, lambda c: c, carry)
            _, l, acc = lax.cond(qi >= 2, win_tail, lambda c: c, carry)
            o_win = acc / l

            g0 = NSA_N_BRANCH * hp
            gate = [gt_ref[g0 + br:g0 + br + 1, pl.ds(q0, t)] for br in range(NSA_N_BRANCH)]
            o = gate[0] * o_cmp[hp] + gate[1] * o_sel + gate[2] * o_win
            ot_ref[hp * hd:(hp + 1) * hd, pl.ds(q0, t)] = o.astype(ot_ref.dtype)
        return 0

    lax.fori_loop(0, seq // t, q_tile, 0)


def _nsa_attention(main_t, k_cmp, v_cmp_t, ks_aug, kw, gates, *, batch, seq, row_vs, row_vw):
    groups, hd, hpg = NSA_KV_GROUPS, NSA_HEAD_DIM, NSA_HPG
    n = batch * seq
    ncr = seq // NSA_CMP_STRIDE
    gh = hpg * hd
    return pl.pallas_call(
        functools.partial(_nsa_kernel, seq=seq),
        grid=(batch, groups),
        in_specs=[
            pl.BlockSpec((gh, seq), lambda b, g: (g, b)),
            pl.BlockSpec((None, ncr, hd), lambda b, g: (b * groups + g, 0, 0)),
            pl.BlockSpec((None, hd, ncr), lambda b, g: (b * groups + g, 0, 0)),
            pl.BlockSpec((None, None, seq, ks_aug.shape[-1]), lambda b, g: (b, g, 0, 0)),
            pl.BlockSpec((hd, seq), lambda b, g: (row_vs // hd + g, b)),
            pl.BlockSpec((None, None, seq, hd), lambda b, g: (b, g, 0, 0)),
            pl.BlockSpec((hd, seq), lambda b, g: (row_vw // hd + g, b)),
            pl.BlockSpec((None, None, hpg * NSA_N_BRANCH, seq), lambda b, g: (b, g, 0, 0)),
        ],
        out_specs=pl.BlockSpec((gh, seq), lambda b, g: (g, b)),
        out_shape=jax.ShapeDtypeStruct((groups * gh, n), BF16),
        compiler_params=_params("parallel", "parallel"),
        name="nsa_attention",
    )(main_t, k_cmp, v_cmp_t, ks_aug, main_t, kw, main_t, gates)


def _nsa_layer(h, g, w_in, gate_b, q_gain, kc_gain, ks_gain, kw_gain, pos_k, pos_v, ck_w1, ck_w2, cv_w1, cv_w2, w_out,
               *, batch, seq):
    heads, groups, hd, hpg = NSA_HEADS, NSA_KV_GROUPS, NSA_HEAD_DIM, NSA_HPG
    assert seq % ATTN_TILE == 0 and NSA_WINDOW == 2 * ATTN_TILE and NSA_CMP_BLOCK == 2 * NSA_CMP_STRIDE
    qw, kvw = heads * hd, groups * hd
    r_kc, r_vc, r_ks, r_vs, r_kw, r_vw, r_g = (qw + i * kvw for i in range(7))
    n_gate = NSA_N_BRANCH * heads
    scale = hd ** -0.5
    ones = jnp.ones((kvw,), F32)
    col = jnp.concatenate([jnp.tile(q_gain, heads) * scale, ones, ones, jnp.tile(ks_gain, groups), ones,
                           jnp.tile(kw_gain, groups), ones, gate_b])
    segs = ((0, qw, "norm"), (r_kc, r_ks, "raw"), (r_ks, r_vs, "norm"), (r_vs, r_kw, "raw"), (r_kw, r_vw, "norm"),
            (r_vw, r_g, "raw"), (r_g, r_g + n_gate, "sigmoid"))
    main_t, gates_t = _proj_t(h, g, w_in.T.astype(BF16), col.reshape(-1, 1), segs, head_dim=hd)

    def token_major(r0):
        return main_t[r0:r0 + kvw].reshape(groups, hd, batch, seq).transpose(2, 0, 3, 1)

    stride = NSA_CMP_STRIDE
    rk = token_major(r_kc).reshape(batch * groups, seq // stride, stride * hd)
    rv = token_major(r_vc).reshape(batch * groups, seq // stride, stride * hd)

    def pos_rows(p):
        return jnp.broadcast_to(p.reshape(1, -1), (8, p.size)).astype(BF16)

    k_cmp, v_cmp = _nsa_compress(rk, rv, pos_rows(pos_k), pos_rows(pos_v), ck_w1.astype(BF16), ck_w2.astype(BF16),
                                 cv_w1.astype(BF16), cv_w2.astype(BF16), kc_gain)
    nsl = seq // NSA_SEL_BLOCK
    onehot = (jnp.arange(seq)[:, None] // NSA_SEL_BLOCK == jnp.arange(nsl)[None, :]).astype(BF16)
    ks_aug = jnp.concatenate([token_major(r_ks), jnp.broadcast_to(onehot, (batch, groups, seq, nsl))], axis=-1)
    gates = gates_t.reshape(groups, hpg * NSA_N_BRANCH, batch, seq).transpose(2, 0, 1, 3)
    o_t = _nsa_attention(main_t, k_cmp, v_cmp.swapaxes(1, 2), ks_aug, token_major(r_kw), gates,
                         batch=batch, seq=seq, row_vs=r_vs, row_vw=r_vw)
    return _out_proj(h, o_t.T, w_out.astype(BF16))


def kernel(x, norm_g, ffn1_wi, ffn1_wo, ffn2_wi, ffn2_wo, moba_w_in, moba_q_gain, moba_k_gain, moba_w_out, lru_w_in, lru_conv_w, lru_conv_b, lru_wa, lru_ba, lru_wx, lru_bx, lru_lam, lru_w_out, nsa_w_in, nsa_gate_b, nsa_q_gain, nsa_kc_gain, nsa_ks_gain, nsa_kw_gain, nsa_pos_k, nsa_pos_v, nsa_ck_w1, nsa_ck_w2, nsa_cv_w1, nsa_cv_w2, nsa_w_out):
    batch, seq, d = x.shape
    n_mixers = 3
    h = x.reshape(batch * seq, d)
    for i in range(norm_g.shape[0]):
        j, kind = divmod(i, n_mixers)
        h = _ffn(h, norm_g[i, 0], ffn1_wi[i].astype(BF16), ffn1_wo[i].astype(BF16))
        if kind == 0:
            h = _moba_layer(h, norm_g[i, 1], moba_w_in[j], moba_q_gain[j], moba_k_gain[j], moba_w_out[j],
                            batch=batch, seq=seq)
        elif kind == 1:
            h = _lru_layer(h, norm_g[i, 1], lru_w_in[j], lru_conv_w[j], lru_conv_b[j], lru_wa[j], lru_ba[j], lru_wx[j],
                           lru_bx[j], lru_lam[j], lru_w_out[j], batch=batch, seq=seq)
        else:
            h = _nsa_layer(h, norm_g[i, 1], nsa_w_in[j], nsa_gate_b[j], nsa_q_gain[j], nsa_kc_gain[j], nsa_ks_gain[j],
                           nsa_kw_gain[j], nsa_pos_k[j], nsa_pos_v[j], nsa_ck_w1[j], nsa_ck_w2[j], nsa_cv_w1[j],
                           nsa_cv_w2[j], nsa_w_out[j], batch=batch, seq=seq)
        h = _ffn(h, norm_g[i, 2], ffn2_wi[i].astype(BF16), ffn2_wo[i].astype(BF16))
    return h.reshape(batch, seq, d)
```

```python
import functools

import jax
import jax.numpy as jnp
from jax import lax
from jax.experimental import pallas as pl
from jax.experimental.pallas import tpu as pltpu

F32 = jnp.float32
BF16 = jnp.bfloat16

EPS = 1e-6
NEG = -1e30
FORCE = 1e30
FFN_RESID = 0.5

MOBA_HEADS = 16
MOBA_BLOCK = 256
MOBA_TOPK = 3

LRU_BLOCKS = 4
CONV_WIDTH = 4
LRU_C = 8.0

NSA_HEADS = 16
NSA_KV_GROUPS = 4
NSA_HEAD_DIM = 64
NSA_HPG = NSA_HEADS // NSA_KV_GROUPS
NSA_CMP_BLOCK = 32
NSA_CMP_STRIDE = 16
NSA_SEL_BLOCK = 64
NSA_TOPN = 16
NSA_LOCAL_BLOCKS = 2
NSA_WINDOW = 512
NSA_N_BRANCH = 3

V7X_VMEM_BYTES = 64 * 1024 * 1024
VMEM_LIMIT_BYTES = V7X_VMEM_BYTES - 8 * 1024 * 1024

ATTN_TILE = 256


def _params(*sem):
    return pltpu.CompilerParams(dimension_semantics=sem, vmem_limit_bytes=VMEM_LIMIT_BYTES)


def _resident(shape):
    nd = len(shape)
    return pl.BlockSpec(shape, lambda *_: (0,) * nd, pipeline_mode=pl.Buffered(1))


def _rms_rows(x, g):
    return x * lax.rsqrt(jnp.mean(x * x, axis=-1, keepdims=True) + EPS) * g


def _gelu_tanh(x):
    c = 0.7978845608028654
    return 0.5 * x * (1.0 + jnp.tanh(c * (x + 0.044715 * (x * x * x))))


def _dot(a, b):
    return jnp.dot(a, b, preferred_element_type=F32)


def _dot_nt(a, b):
    return lax.dot_general(a, b, (((1,), (1,)), ((), ())), preferred_element_type=F32)


def _ffn_kernel(x_ref, g_ref, wi_ref, wo_ref, o_ref, *, d_ff, chunk):
    x = x_ref[...]
    xn = _rms_rows(x, g_ref[...]).astype(BF16)
    acc = jnp.zeros(x.shape, F32)
    for c0 in range(0, d_ff, chunk):
        gate = _dot(xn, wi_ref[:, c0:c0 + chunk])
        up = _dot(xn, wi_ref[:, d_ff + c0:d_ff + c0 + chunk])
        act = (gate * jax.nn.sigmoid(gate) * up).astype(BF16)
        acc = acc + _dot(act, wo_ref[c0:c0 + chunk, :])
    o_ref[...] = x + FFN_RESID * acc


def _ffn(h, g, wi, wo, *, tm=512, chunk=256):
    n, d = h.shape
    d_ff = wo.shape[0]
    return pl.pallas_call(
        functools.partial(_ffn_kernel, d_ff=d_ff, chunk=chunk),
        grid=(n // tm,),
        in_specs=[
            pl.BlockSpec((tm, d), lambda i: (i, 0)),
            _resident((1, d)),
            _resident(wi.shape),
            _resident(wo.shape),
        ],
        out_specs=pl.BlockSpec((tm, d), lambda i: (i, 0)),
        out_shape=jax.ShapeDtypeStruct((n, d), F32),
        compiler_params=_params("parallel"),
        name="ffn",
    )(h, g.reshape(1, d), wi, wo)


def _proj_t_kernel(x_ref, g_ref, wt_ref, col_ref, o_ref, *aux_refs, segs, head_dim):
    xn = _rms_rows(x_ref[...], g_ref[...]).astype(BF16)
    tm = xn.shape[0]
    for r0, r1, mode in segs:
        acc = _dot_nt(wt_ref[r0:r1, :], xn)
        if mode == "raw":
            o_ref[r0:r1, :] = acc.astype(o_ref.dtype)
        elif mode == "norm":
            nh = (r1 - r0) // head_dim
            a3 = acc.reshape(nh, head_dim, tm)
            ms = jnp.mean(a3 * a3, axis=1, keepdims=True)
            y = a3 * lax.rsqrt(ms + EPS) * col_ref[r0:r1, :].reshape(nh, head_dim, 1)
            o_ref[r0:r1, :] = y.reshape(r1 - r0, tm).astype(o_ref.dtype)
        else:
            (aux_ref,) = aux_refs
            aux_ref[...] = jax.nn.sigmoid(acc + col_ref[r0:r1, :])


def _proj_t(h, g, wt, col, segs, *, head_dim, tm=512):
    n, d = h.shape
    rows = wt.shape[0]
    n_main = max(r1 for _, r1, mode in segs if mode != "sigmoid")
    aux = [(r0, r1) for r0, r1, mode in segs if mode == "sigmoid"]
    out_shape = [jax.ShapeDtypeStruct((n_main, n), BF16)]
    out_specs = [pl.BlockSpec((n_main, tm), lambda i: (0, i))]
    if aux:
        ((a0, a1),) = aux
        out_shape.append(jax.ShapeDtypeStruct((a1 - a0, n), F32))
        out_specs.append(pl.BlockSpec((a1 - a0, tm), lambda i: (0, i)))
    return pl.pallas_call(
        functools.partial(_proj_t_kernel, segs=segs, head_dim=head_dim),
        grid=(n // tm,),
        in_specs=[
            pl.BlockSpec((tm, d), lambda i: (i, 0)),
            _resident((1, d)),
            _resident((rows, d)),
            _resident((rows, 1)),
        ],
        out_specs=out_specs,
        out_shape=out_shape,
        compiler_params=_params("parallel"),
        name="proj_t",
    )(h, g.reshape(1, d), wt, col)


def _out_proj_kernel(h_ref, a_ref, w_ref, o_ref):
    o_ref[...] = h_ref[...] + _dot(a_ref[...], w_ref[...])


def _out_proj(h, a, w, *, tm=1024):
    n, d = h.shape
    k = a.shape[1]
    return pl.pallas_call(
        _out_proj_kernel,
        grid=(n // tm,),
        in_specs=[
            pl.BlockSpec((tm, d), lambda i: (i, 0)),
            pl.BlockSpec((tm, k), lambda i: (i, 0)),
            _resident((k, d)),
        ],
        out_specs=pl.BlockSpec((tm, d), lambda i: (i, 0)),
        out_shape=jax.ShapeDtypeStruct((n, d), F32),
        compiler_params=_params("parallel"),
        name="out_proj",
    )(h, a, w)


ONES_ROWS = 16


def _flash_pipelined(ss_ref, n_past, diag_tile, score_fn, value_fn, hd, width):
    n_chain = ss_ref.shape[0]
    last = jnp.maximum(n_past - 1, 0)
    init = tuple((jnp.full((1, width), NEG, F32), jnp.zeros((hd + ONES_ROWS, width), F32)) for _ in range(n_chain))

    def step(i, carries):
        nxt = jnp.minimum(i, last)
        cur = jnp.where(i == 0, diag_tile, i - 1)
        out = []
        for c in range(n_chain):
            s = ss_ref[c]
            m, acc = carries[c]
            m_new = jnp.maximum(m, jnp.max(s, axis=0, keepdims=True))
            p = jnp.exp2(s - m_new).astype(BF16)
            ss_ref[c] = score_fn(c, nxt)
            out.append((m_new, jnp.exp2(m - m_new) * acc + _dot(value_fn(c, cur), p)))
        return tuple(out)

    return lax.fori_loop(0, n_past + 1, step, init)


def _moba_kernel(qt_ref, ka_ref, vt_ref, ot_ref, ss_ref, *, seq, hb, hd):
    t = ATTN_TILE
    nb = seq // t
    kw = ka_ref.shape[-1]
    avg = jnp.where(lax.broadcasted_iota(jnp.int32, (nb, seq), 1) // t
                    == lax.broadcasted_iota(jnp.int32, (nb, seq), 0), 1.0 / t, 0.0).astype(BF16)
    km = []
    for h in range(hb):
        kmean = _dot(avg, ka_ref[h])
        km_hi = kmean.astype(BF16)
        km.append((km_hi, (kmean - km_hi.astype(F32)).astype(BF16)))
    blk = lax.broadcasted_iota(jnp.int32, (nb, t), 0)
    causal = (lax.broadcasted_iota(jnp.int32, (t, t), 0) <= lax.broadcasted_iota(jnp.int32, (t, t), 1))
    rows = [slice(h * hd, (h + 1) * hd) for h in range(hb)]
    ones = jnp.ones((ONES_ROWS, t), BF16)

    def q_tile(qi, _):
        q0 = pl.multiple_of(qi * t, t)
        qs = [qt_ref[r, pl.ds(q0, t)] for r in rows]
        qz = [jnp.concatenate([q, jnp.zeros((kw - hd, t), BF16)], axis=0) for q in qs]
        gates = [_dot(km[h][0], qz[h]) + _dot(km[h][1], qz[h]) for h in range(hb)]
        for h in range(hb):
            ss_ref[h] = jnp.where(causal, _dot(ka_ref[h, pl.ds(q0, t), :], qz[h]), NEG)
        past = blk < qi
        qa = []
        for h in range(hb):
            gate = jnp.where(past, gates[h], NEG)
            sel = jnp.zeros((nb, t), jnp.bool_)
            for _ in range(min(MOBA_TOPK, nb)):
                top = jnp.max(gate, axis=0, keepdims=True)
                first = jnp.min(jnp.where(gate == top, blk, nb), axis=0, keepdims=True)
                hit = blk == first
                sel = jnp.logical_or(sel, hit)
                gate = jnp.where(hit, -jnp.inf, gate)
            bias = jnp.where(jnp.logical_and(sel, past), 0.0, NEG).astype(BF16)
            qa.append(jnp.concatenate([qs[h], bias, jnp.zeros((kw - hd - nb, t), BF16)], axis=0))

        def score(h, j):
            return _dot(ka_ref[h, pl.ds(pl.multiple_of(j * t, t), t), :], qa[h])

        def value(h, j):
            return jnp.concatenate([vt_ref[rows[h], pl.ds(pl.multiple_of(j * t, t), t)], ones], axis=0)

        carries = _flash_pipelined(ss_ref, qi, qi, score, value, hd, t)
        for h in range(hb):
            _, acc = carries[h]
            ot_ref[rows[h], pl.ds(q0, t)] = (acc[:hd] / acc[hd:hd + 1]).astype(ot_ref.dtype)
        return 0

    lax.fori_loop(0, nb, q_tile, 0)


MOBA_HEADS_PER_STEP = 8


def _moba_attention(qkv_t, k_aug, *, batch, seq, heads, head_dim):
    n = batch * seq
    hb = MOBA_HEADS_PER_STEP
    hg = heads // hb
    return pl.pallas_call(
        functools.partial(_moba_kernel, seq=seq, hb=hb, hd=head_dim),
        grid=(batch, hg),
        in_specs=[
            pl.BlockSpec((hb * head_dim, seq), lambda b, h: (h, b)),
            pl.BlockSpec((None, hb, seq, k_aug.shape[-1]), lambda b, h: (b, h, 0, 0)),
            pl.BlockSpec((hb * head_dim, seq), lambda b, h: (2 * hg + h, b)),
        ],
        out_specs=pl.BlockSpec((hb * head_dim, seq), lambda b, h: (h, b)),
        out_shape=jax.ShapeDtypeStruct((heads * head_dim, n), BF16),
        scratch_shapes=[pltpu.VMEM((hb, ATTN_TILE, ATTN_TILE), F32)],
        compiler_params=_params("parallel", "parallel"),
        name="moba_attention",
    )(qkv_t, k_aug, qkv_t)


LANES = 128
LOG2E = 1.4426950408889634


def _block_onehot(seq, block, width):
    return (jnp.arange(seq)[:, None] // block == jnp.arange(width)[None, :]).astype(BF16)


def _moba_layer(h, g, w_in, q_gain, k_gain, w_out, *, batch, seq):
    heads = MOBA_HEADS
    hd = w_in.shape[1] // (3 * heads)
    hw = heads * hd
    assert seq % MOBA_BLOCK == 0 and MOBA_BLOCK == ATTN_TILE and hd + seq // MOBA_BLOCK <= LANES
    col = jnp.concatenate([jnp.tile(q_gain, heads) * (hd ** -0.5 * LOG2E), jnp.tile(k_gain, heads), jnp.ones((hw,), F32)])
    segs = ((0, hw, "norm"), (hw, 2 * hw, "norm"), (2 * hw, 3 * hw, "raw"))
    (qkv_t,) = _proj_t(h, g, w_in.T.astype(BF16), col.reshape(-1, 1), segs, head_dim=hd)
    kn = qkv_t[hw:2 * hw].reshape(heads, hd, batch, seq).transpose(2, 0, 3, 1)
    onehot = _block_onehot(seq, MOBA_BLOCK, LANES - hd)
    k_aug = jnp.concatenate([kn, jnp.broadcast_to(onehot, (batch, heads, seq, LANES - hd))], axis=-1)
    o_t = _moba_attention(qkv_t, k_aug, batch=batch, seq=seq, heads=heads, head_dim=hd)
    return _out_proj(h, o_t.T, w_out.astype(BF16))


def _lru_proj_kernel(x_ref, g_ref, w_ref, xb_ref, y_ref, *, width):
    xn = _rms_rows(x_ref[...], g_ref[...]).astype(BF16)
    xb_ref[...] = _dot(xn, w_ref[:, :width])
    y_ref[...] = _gelu_tanh(_dot(xn, w_ref[:, width:]))


def _lru_proj(h, g, w, *, tm=512):
    n, d = h.shape
    width = w.shape[1] // 2
    return pl.pallas_call(
        functools.partial(_lru_proj_kernel, width=width),
        grid=(n // tm,),
        in_specs=[pl.BlockSpec((tm, d), lambda i: (i, 0)), _resident((1, d)), _resident(w.shape)],
        out_specs=[pl.BlockSpec((tm, width), lambda i: (i, 0))] * 2,
        out_shape=[jax.ShapeDtypeStruct((n, width), F32)] * 2,
        compiler_params=_params("parallel"),
        name="lru_proj",
    )(h, g.reshape(1, d), w)


LRU_CONV_PAD = 8


def _lru_kernel(xb_ref, y_ref, cw_ref, cb_ref, wa_ref, ba_ref, wx_ref, bx_ref, lam_ref, o_ref,
                xpad, a_s, b_s, h_s, h_carry, *, tile, width):
    si = pl.program_id(1)
    pad = LRU_CONV_PAD

    @pl.when(si == 0)
    def _():
        xpad[0:pad, :] = jnp.zeros((pad, width), F32)
        h_carry[...] = jnp.zeros((1, width), F32)

    xpad[pad:pad + tile, :] = xb_ref[...]
    xc = cb_ref[...] + cw_ref[0:1, :] * xpad[pad - CONV_WIDTH + 1:pad - CONV_WIDTH + 1 + tile, :]
    for k in range(1, CONV_WIDTH):
        off = pad - CONV_WIDTH + 1 + k
        xc = xc + cw_ref[k:k + 1, :] * xpad[off:off + tile, :]
    xcb = xc.astype(BF16)
    first = (lax.broadcasted_iota(jnp.int32, (tile, 1), 0) + si * tile) == 0
    bw = width // LRU_BLOCKS
    for n in range(LRU_BLOCKS):
        sl = slice(n * bw, (n + 1) * bw)
        r = jax.nn.sigmoid(_dot(xcb[:, sl], wa_ref[n]) + ba_ref[:, sl])
        i = jax.nn.sigmoid(_dot(xcb[:, sl], wx_ref[n]) + bx_ref[:, sl])
        z = -lam_ref[:, sl]
        softplus = jnp.maximum(z, 0.0) + jnp.log1p(jnp.exp(-jnp.abs(z)))
        a = jnp.exp(-LRU_C * r * softplus)
        mult = jnp.where(first, 1.0, jnp.sqrt(1.0 - a * a))
        a_s[:, sl] = a
        b_s[:, sl] = mult * i * xc[:, sl]

    def step(t, h):
        h = a_s[pl.ds(t, 1), :] * h + b_s[pl.ds(t, 1), :]
        h_s[pl.ds(t, 1), :] = h
        return h

    h_carry[...] = lax.fori_loop(0, tile, step, h_carry[...], unroll=8)
    o_ref[...] = (h_s[...] * y_ref[...]).astype(o_ref.dtype)
    xpad[0:pad, :] = xpad[tile:tile + pad, :]


def _lru_recurrence(xb, y, cw, cb, wa, ba, wx, bx, lam, *, batch, seq, tile=512):
    n, width = xb.shape
    nt = seq // tile
    row = lambda b, s: (b * nt + s, 0)
    vec = _resident((1, width))
    return pl.pallas_call(
        functools.partial(_lru_kernel, tile=tile, width=width),
        grid=(batch, nt),
        in_specs=[
            pl.BlockSpec((tile, width), row),
            pl.BlockSpec((tile, width), row),
            _resident(cw.shape), vec,
            _resident(wa.shape), vec,
            _resident(wx.shape), vec,
            vec,
        ],
        out_specs=pl.BlockSpec((tile, width), row),
        out_shape=jax.ShapeDtypeStruct((n, width), BF16),
        scratch_shapes=[
            pltpu.VMEM((tile + LRU_CONV_PAD, width), F32),
            pltpu.VMEM((tile, width), F32),
            pltpu.VMEM((tile, width), F32),
            pltpu.VMEM((tile, width), F32),
            pltpu.VMEM((1, width), F32),
        ],
        compiler_params=_params("parallel", "arbitrary"),
        name="lru_recurrence",
    )(xb, y, cw, cb.reshape(1, width), wa, ba.reshape(1, width), wx, bx.reshape(1, width), lam.reshape(1, width))


def _lru_layer(h, g, w_in, conv_w, conv_b, wa, ba, wx, bx, lam, w_out, *, batch, seq):
    xb, y = _lru_proj(h, g, w_in.astype(BF16))
    gated = _lru_recurrence(xb, y, conv_w.reshape(CONV_WIDTH, -1), conv_b, wa.astype(BF16), ba, wx.astype(BF16), bx, lam,
                            batch=batch, seq=seq)
    return _out_proj(h, gated, w_out.astype(BF16))


def _nsa_compress_kernel(rk_ref, rv_ref, pk_ref, pv_ref, kw1_ref, kw2_ref, vw1_ref, vw2_ref, kg_ref, ko_ref, vo_ref):
    def compress(r_ref, pos_ref, w1_ref, w2_ref):
        r = r_ref[...]
        half = r.shape[1]
        rows = r.shape[0]
        lo = _dot(r, w1_ref[:half, :])
        hi = _dot(r, w1_ref[half:, :])
        hi_next = pltpu.roll(hi, shift=rows - 1, axis=0)
        pos = _dot(pos_ref[...], w1_ref[...])[0:1, :]
        hidden = _gelu_tanh(lo + hi_next + pos)
        return _dot(hidden.astype(BF16), w2_ref[...])

    ko_ref[...] = _rms_rows(compress(rk_ref, pk_ref, kw1_ref, kw2_ref), kg_ref[...]).astype(ko_ref.dtype)
    vo_ref[...] = compress(rv_ref, pv_ref, vw1_ref, vw2_ref).astype(vo_ref.dtype)


def _nsa_compress(rk, rv, pos_k, pos_v, kw1, kw2, vw1, vw2, k_gain):
    bg, rows, wide = rk.shape
    hd = kw2.shape[1]
    blk = pl.BlockSpec((None, rows, wide), lambda i: (i, 0, 0))
    out = pl.BlockSpec((None, rows, hd), lambda i: (i, 0, 0))
    return pl.pallas_call(
        _nsa_compress_kernel,
        grid=(bg,),
        in_specs=[blk, blk, _resident(pos_k.shape), _resident(pos_v.shape), _resident(kw1.shape), _resident(kw2.shape),
                  _resident(vw1.shape), _resident(vw2.shape), _resident((1, hd))],
        out_specs=[out, out],
        out_shape=[jax.ShapeDtypeStruct((bg, rows, hd), BF16)] * 2,
        compiler_params=_params("parallel"),
        name="nsa_compress",
    )(rk, rv, pos_k, pos_v, kw1, kw2, vw1, vw2, k_gain.reshape(1, hd))


def _nsa_kernel(qt_ref, kc_ref, vct_ref, ks_ref, vst_ref, kw_ref, vwt_ref, gt_ref, ot_ref, ss_ref, *, seq):
    t = ATTN_TILE
    hd = NSA_HEAD_DIM
    ncr = seq // NSA_CMP_STRIDE
    nsl = seq // NSA_SEL_BLOCK
    n_top = min(NSA_TOPN, nsl)
    jn = lax.broadcasted_iota(jnp.int32, (nsl, ncr), 0) * NSA_SEL_BLOCK
    cn = lax.broadcasted_iota(jnp.int32, (nsl, ncr), 1) * NSA_CMP_STRIDE
    ov_t = jnp.where(jnp.logical_and(cn < jn + NSA_SEL_BLOCK, cn + NSA_CMP_BLOCK > jn), 1.0, 0.0).astype(BF16)
    tw = NSA_HPG * t
    diff = lax.broadcasted_iota(jnp.int32, (t, tw), 0) - (lax.broadcasted_iota(jnp.int32, (t, tw), 1) & (t - 1))
    causal = (lax.broadcasted_iota(jnp.int32, (t, t), 0) <= lax.broadcasted_iota(jnp.int32, (t, t), 1))
    cmp_end = lax.broadcasted_iota(jnp.int32, (ncr, tw), 0) * NSA_CMP_STRIDE + (NSA_CMP_BLOCK - 1)
    cmp_q = lax.broadcasted_iota(jnp.int32, (ncr, tw), 1) & (t - 1)
    blk_j = lax.broadcasted_iota(jnp.int32, (nsl, t), 0)
    ones = jnp.ones((ONES_ROWS, t), BF16)

    def q_tile(qi, _):
        q0 = pl.multiple_of(qi * t, t)
        qs = [qt_ref[hp * hd:(hp + 1) * hd, pl.ds(q0, t)] for hp in range(NSA_HPG)]
        q = jnp.concatenate(qs, axis=1)

        k1 = pl.multiple_of(jnp.maximum(qi - 1, 0) * t, t)
        k2 = pl.multiple_of(jnp.maximum(qi - 2, 0) * t, t)
        s_cmp = _dot(kc_ref[...], q)
        s_win = [_dot(kw_ref[pl.ds(k0, t), :], q) for k0 in (q0, k1, k2)]

        m_c = cmp_end <= q0 + cmp_q
        s = jnp.where(m_c, s_cmp, NEG)
        e = jnp.where(m_c, jnp.exp2(s - jnp.max(s, axis=0, keepdims=True)), 0.0)
        den = jnp.sum(e, axis=0, keepdims=True)
        p = e / jnp.where(den > 0.0, den, 1.0)
        o_cmp = _dot(vct_ref[...], p.astype(BF16))
        p_sum = p[:, 0:t]
        for hp in range(1, NSA_HPG):
            p_sum = p_sum + p[:, hp * t:(hp + 1) * t]
        ps_hi = p_sum.astype(BF16)
        ps_lo = (p_sum - ps_hi.astype(F32)).astype(BF16)
        imp = _dot(ov_t, ps_hi) + _dot(ov_t, ps_lo)
        qpos = q0 + lax.broadcasted_iota(jnp.int32, (nsl, t), 1)
        back = qpos // NSA_SEL_BLOCK - blk_j
        forced = jnp.logical_or(blk_j == 0, jnp.logical_and(back >= 0, back < NSA_LOCAL_BLOCKS))
        valid = blk_j * NSA_SEL_BLOCK <= qpos
        score = jnp.where(valid, jnp.where(forced, FORCE, imp), NEG)
        rank = jnp.zeros((nsl, t), jnp.int32)
        for m in range(nsl):
            row = score[m:m + 1, :]
            rank = rank + jnp.where(blk_j > m, jnp.where(row >= score, 1, 0), jnp.where(row > score, 1, 0))
        sel = jnp.logical_and(rank < n_top, valid)
        bias = jnp.where(sel, 0.0, NEG).astype(BF16)

        lo1 = jnp.where(qi >= 1, -t, t)
        lo2 = jnp.where(qi >= 2, 0, t)
        s_win = [jnp.where(diff <= 0, s_win[0], NEG), jnp.where(diff > lo1, s_win[1], NEG),
                 jnp.where(diff > lo2, s_win[2], NEG)]
        m_w = functools.reduce(jnp.maximum, [jnp.max(s, axis=0, keepdims=True) for s in s_win])
        acc_w = functools.reduce(jnp.add, [
            _dot(jnp.concatenate([vwt_ref[:, pl.ds(k0, t)], ones], axis=0), jnp.exp2(s - m_w).astype(BF16))
            for s, k0 in zip(s_win, (q0, k1, k2))])
        o_win = acc_w[:hd] / acc_w[hd:hd + 1]

        qa = [jnp.concatenate([qh, bias], axis=0) for qh in qs]
        for hp in range(NSA_HPG):
            ss_ref[hp] = jnp.where(causal, _dot(ks_ref[pl.ds(q0, t), :], qa[hp]), NEG)

        def score(hp, j):
            return _dot(ks_ref[pl.ds(pl.multiple_of(j * t, t), t), :], qa[hp])

        def value(hp, j):
            return jnp.concatenate([vst_ref[:, pl.ds(pl.multiple_of(j * t, t), t)], ones], axis=0)

        carries = _flash_pipelined(ss_ref, qi, qi, score, value, hd, t)
        for hp in range(NSA_HPG):
            _, acc = carries[hp]
            cols = slice(hp * t, (hp + 1) * t)
            gate = [gt_ref[NSA_N_BRANCH * hp + br:NSA_N_BRANCH * hp + br + 1, pl.ds(q0, t)] for br in range(NSA_N_BRANCH)]
            o = gate[0] * o_cmp[:, cols] + gate[1] * (acc[:hd] / acc[hd:hd + 1]) + gate[2] * o_win[:, cols]
            ot_ref[hp * hd:(hp + 1) * hd, pl.ds(q0, t)] = o.astype(ot_ref.dtype)
        return 0

    lax.fori_loop(0, seq // t, q_tile, 0)


def _nsa_attention(main_t, k_cmp, v_cmp_t, ks_aug, kw, gates, *, batch, seq, row_vs, row_vw):
    groups, hd, hpg = NSA_KV_GROUPS, NSA_HEAD_DIM, NSA_HPG
    n = batch * seq
    ncr = seq // NSA_CMP_STRIDE
    gh = hpg * hd
    return pl.pallas_call(
        functools.partial(_nsa_kernel, seq=seq),
        grid=(batch, groups),
        in_specs=[
            pl.BlockSpec((gh, seq), lambda b, g: (g, b)),
            pl.BlockSpec((None, ncr, hd), lambda b, g: (b * groups + g, 0, 0)),
            pl.BlockSpec((None, hd, ncr), lambda b, g: (b * groups + g, 0, 0)),
            pl.BlockSpec((None, None, seq, ks_aug.shape[-1]), lambda b, g: (b, g, 0, 0)),
            pl.BlockSpec((hd, seq), lambda b, g: (row_vs // hd + g, b)),
            pl.BlockSpec((None, None, seq, hd), lambda b, g: (b, g, 0, 0)),
            pl.BlockSpec((hd, seq), lambda b, g: (row_vw // hd + g, b)),
            pl.BlockSpec((None, None, hpg * NSA_N_BRANCH, seq), lambda b, g: (b, g, 0, 0)),
        ],
        out_specs=pl.BlockSpec((gh, seq), lambda b, g: (g, b)),
        out_shape=jax.ShapeDtypeStruct((groups * gh, n), BF16),
        scratch_shapes=[pltpu.VMEM((hpg, ATTN_TILE, ATTN_TILE), F32)],
        compiler_params=_params("parallel", "parallel"),
        name="nsa_attention",
    )(main_t, k_cmp, v_cmp_t, ks_aug, main_t, kw, main_t, gates)


def _nsa_layer(h, g, w_in, gate_b, q_gain, kc_gain, ks_gain, kw_gain, pos_k, pos_v, ck_w1, ck_w2, cv_w1, cv_w2, w_out,
               *, batch, seq):
    heads, groups, hd, hpg = NSA_HEADS, NSA_KV_GROUPS, NSA_HEAD_DIM, NSA_HPG
    assert seq % ATTN_TILE == 0 and NSA_WINDOW == 2 * ATTN_TILE and NSA_CMP_BLOCK == 2 * NSA_CMP_STRIDE
    qw, kvw = heads * hd, groups * hd
    r_kc, r_vc, r_ks, r_vs, r_kw, r_vw, r_g = (qw + i * kvw for i in range(7))
    n_gate = NSA_N_BRANCH * heads
    ones = jnp.ones((kvw,), F32)
    col = jnp.concatenate([jnp.tile(q_gain, heads) * (hd ** -0.5 * LOG2E), ones, ones, jnp.tile(ks_gain, groups), ones,
                           jnp.tile(kw_gain, groups), ones, gate_b])
    segs = ((0, qw, "norm"), (r_kc, r_ks, "raw"), (r_ks, r_vs, "norm"), (r_vs, r_kw, "raw"), (r_kw, r_vw, "norm"),
            (r_vw, r_g, "raw"), (r_g, r_g + n_gate, "sigmoid"))
    main_t, gates_t = _proj_t(h, g, w_in.T.astype(BF16), col.reshape(-1, 1), segs, head_dim=hd)

    def token_major(r0):
        return main_t[r0:r0 + kvw].reshape(groups, hd, batch, seq).transpose(2, 0, 3, 1)

    stride = NSA_CMP_STRIDE
    rk = token_major(r_kc).reshape(batch * groups, seq // stride, stride * hd)
    rv = token_major(r_vc).reshape(batch * groups, seq // stride, stride * hd)

    def pos_rows(p):
        return jnp.broadcast_to(p.reshape(1, -1), (8, p.size)).astype(BF16)

    k_cmp, v_cmp = _nsa_compress(rk, rv, pos_rows(pos_k), pos_rows(pos_v), ck_w1.astype(BF16), ck_w2.astype(BF16),
                                 cv_w1.astype(BF16), cv_w2.astype(BF16), kc_gain)
    nsl = seq // NSA_SEL_BLOCK
    onehot = _block_onehot(seq, NSA_SEL_BLOCK, nsl)
    ks_aug = jnp.concatenate([token_major(r_ks), jnp.broadcast_to(onehot, (batch, groups, seq, nsl))], axis=-1)
    gates = gates_t.reshape(groups, hpg * NSA_N_BRANCH, batch, seq).transpose(2, 0, 1, 3)
    o_t = _nsa_attention(main_t, k_cmp, v_cmp.swapaxes(1, 2), ks_aug, token_major(r_kw), gates,
                         batch=batch, seq=seq, row_vs=r_vs, row_vw=r_vw)
    return _out_proj(h, o_t.T, w_out.astype(BF16))


def kernel(x, norm_g, ffn1_wi, ffn1_wo, ffn2_wi, ffn2_wo, moba_w_in, moba_q_gain, moba_k_gain, moba_w_out, lru_w_in, lru_conv_w, lru_conv_b, lru_wa, lru_ba, lru_wx, lru_bx, lru_lam, lru_w_out, nsa_w_in, nsa_gate_b, nsa_q_gain, nsa_kc_gain, nsa_ks_gain, nsa_kw_gain, nsa_pos_k, nsa_pos_v, nsa_ck_w1, nsa_ck_w2, nsa_cv_w1, nsa_cv_w2, nsa_w_out):
    batch, seq, d = x.shape
    n_mixers = 3
    h = x.reshape(batch * seq, d)
    for i in range(norm_g.shape[0]):
        j, kind = divmod(i, n_mixers)
        h = _ffn(h, norm_g[i, 0], ffn1_wi[i].astype(BF16), ffn1_wo[i].astype(BF16))
        if kind == 0:
            h = _moba_layer(h, norm_g[i, 1], moba_w_in[j], moba_q_gain[j], moba_k_gain[j], moba_w_out[j],
                            batch=batch, seq=seq)
        elif kind == 1:
            h = _lru_layer(h, norm_g[i, 1], lru_w_in[j], lru_conv_w[j], lru_conv_b[j], lru_wa[j], lru_ba[j], lru_wx[j],
                           lru_bx[j], lru_lam[j], lru_w_out[j], batch=batch, seq=seq)
        else:
            h = _nsa_layer(h, norm_g[i, 1], nsa_w_in[j], nsa_gate_b[j], nsa_q_gain[j], nsa_kc_gain[j], nsa_ks_gain[j],
                           nsa_kw_gain[j], nsa_pos_k[j], nsa_pos_v[j], nsa_ck_w1[j], nsa_ck_w2[j], nsa_cv_w1[j],
                           nsa_cv_w2[j], nsa_w_out[j], batch=batch, seq=seq)
        h = _ffn(h, norm_g[i, 2], ffn2_wi[i].astype(BF16), ffn2_wo[i].astype(BF16))
    return h.reshape(batch, seq, d)
```

```python
import functools

import jax
import jax.numpy as jnp
from jax import lax
from jax.experimental import pallas as pl
from jax.experimental.pallas import tpu as pltpu

F32 = jnp.float32
BF16 = jnp.bfloat16

EPS = 1e-6
NEG = -1e30
FORCE = 1e30
FFN_RESID = 0.5

MOBA_HEADS = 16
MOBA_BLOCK = 256
MOBA_TOPK = 3

LRU_BLOCKS = 4
CONV_WIDTH = 4
LRU_C = 8.0

NSA_HEADS = 16
NSA_KV_GROUPS = 4
NSA_HEAD_DIM = 64
NSA_HPG = NSA_HEADS // NSA_KV_GROUPS
NSA_CMP_BLOCK = 32
NSA_CMP_STRIDE = 16
NSA_SEL_BLOCK = 64
NSA_TOPN = 16
NSA_LOCAL_BLOCKS = 2
NSA_WINDOW = 512
NSA_N_BRANCH = 3

V7X_VMEM_BYTES = 64 * 1024 * 1024
VMEM_LIMIT_BYTES = V7X_VMEM_BYTES - 8 * 1024 * 1024

LANES = 128
SUBLANES = 8
LOG2E = 1.4426950408889634

ATTN_TILE = 256


def _params(*sem):
    return pltpu.CompilerParams(dimension_semantics=sem, vmem_limit_bytes=VMEM_LIMIT_BYTES)


def _resident(shape):
    nd = len(shape)
    return pl.BlockSpec(shape, lambda *_: (0,) * nd, pipeline_mode=pl.Buffered(1))


def _rms_rows(x, g):
    return x * lax.rsqrt(jnp.mean(x * x, axis=-1, keepdims=True) + EPS) * g


def _gelu_tanh(x):
    c = 0.7978845608028654
    return 0.5 * x * (1.0 + jnp.tanh(c * (x + 0.044715 * (x * x * x))))


def _dot(a, b):
    return jnp.dot(a, b, preferred_element_type=F32)


def _dot_nt(a, b):
    return lax.dot_general(a, b, (((1,), (1,)), ((), ())), preferred_element_type=F32)


def _ffn_kernel(x_ref, g_ref, wi_ref, wo_ref, o_ref, *, d_ff, chunk):
    x = x_ref[...]
    xn = _rms_rows(x, g_ref[...]).astype(BF16)
    acc = jnp.zeros(x.shape, F32)
    for c0 in range(0, d_ff, chunk):
        gate = _dot(xn, wi_ref[:, c0:c0 + chunk])
        up = _dot(xn, wi_ref[:, d_ff + c0:d_ff + c0 + chunk])
        act = (gate * jax.nn.sigmoid(gate) * up).astype(BF16)
        acc = acc + _dot(act, wo_ref[c0:c0 + chunk, :])
    o_ref[...] = x + FFN_RESID * acc


def _ffn(h, g, wi, wo, *, tm=512, chunk=256):
    n, d = h.shape
    d_ff = wo.shape[0]
    return pl.pallas_call(
        functools.partial(_ffn_kernel, d_ff=d_ff, chunk=chunk),
        grid=(n // tm,),
        in_specs=[
            pl.BlockSpec((tm, d), lambda i: (i, 0)),
            _resident((1, d)),
            _resident(wi.shape),
            _resident(wo.shape),
        ],
        out_specs=pl.BlockSpec((tm, d), lambda i: (i, 0)),
        out_shape=jax.ShapeDtypeStruct((n, d), F32),
        compiler_params=_params("parallel"),
        name="ffn",
    )(h, g.reshape(1, d), wi, wo)


def _proj_t_kernel(x_ref, g_ref, wt_ref, col_ref, o_ref, *aux_refs, segs, head_dim):
    xn = _rms_rows(x_ref[...], g_ref[...]).astype(BF16)
    tm = xn.shape[0]
    for r0, r1, mode in segs:
        acc = _dot_nt(wt_ref[r0:r1, :], xn)
        if mode == "raw":
            o_ref[r0:r1, :] = acc.astype(o_ref.dtype)
        elif mode == "norm":
            nh = (r1 - r0) // head_dim
            a3 = acc.reshape(nh, head_dim, tm)
            ms = jnp.mean(a3 * a3, axis=1, keepdims=True)
            y = a3 * lax.rsqrt(ms + EPS) * col_ref[r0:r1, :].reshape(nh, head_dim, 1)
            o_ref[r0:r1, :] = y.reshape(r1 - r0, tm).astype(o_ref.dtype)
        else:
            (aux_ref,) = aux_refs
            aux_ref[...] = jax.nn.sigmoid(acc + col_ref[r0:r1, :])


def _proj_t(h, g, wt, col, segs, *, head_dim, tm=512):
    n, d = h.shape
    rows = wt.shape[0]
    n_main = max(r1 for _, r1, mode in segs if mode != "sigmoid")
    aux = [(r0, r1) for r0, r1, mode in segs if mode == "sigmoid"]
    out_shape = [jax.ShapeDtypeStruct((n_main, n), BF16)]
    out_specs = [pl.BlockSpec((n_main, tm), lambda i: (0, i))]
    if aux:
        ((a0, a1),) = aux
        out_shape.append(jax.ShapeDtypeStruct((a1 - a0, n), F32))
        out_specs.append(pl.BlockSpec((a1 - a0, tm), lambda i: (0, i)))
    return pl.pallas_call(
        functools.partial(_proj_t_kernel, segs=segs, head_dim=head_dim),
        grid=(n // tm,),
        in_specs=[
            pl.BlockSpec((tm, d), lambda i: (i, 0)),
            _resident((1, d)),
            _resident((rows, d)),
            _resident((rows, 1)),
        ],
        out_specs=out_specs,
        out_shape=out_shape,
        compiler_params=_params("parallel"),
        name="proj_t",
    )(h, g.reshape(1, d), wt, col)


def _out_proj_kernel(h_ref, a_ref, w_ref, o_ref):
    o_ref[...] = h_ref[...] + _dot(a_ref[...], w_ref[...])


def _out_proj(h, a, w, *, tm=1024):
    n, d = h.shape
    k = a.shape[1]
    return pl.pallas_call(
        _out_proj_kernel,
        grid=(n // tm,),
        in_specs=[
            pl.BlockSpec((tm, d), lambda i: (i, 0)),
            pl.BlockSpec((tm, k), lambda i: (i, 0)),
            _resident((k, d)),
        ],
        out_specs=pl.BlockSpec((tm, d), lambda i: (i, 0)),
        out_shape=jax.ShapeDtypeStruct((n, d), F32),
        compiler_params=_params("parallel"),
        name="out_proj",
    )(h, a, w)


def _out_proj_t_kernel(h_ref, at_ref, w_ref, o_ref):
    o_ref[...] = h_ref[...] + lax.dot_general(at_ref[...], w_ref[...], (((0,), (0,)), ((), ())),
                                               preferred_element_type=F32)


def _out_proj_t(h, a_t, w, *, tm=1024):
    n, d = h.shape
    k = a_t.shape[0]
    return pl.pallas_call(
        _out_proj_t_kernel,
        grid=(n // tm,),
        in_specs=[
            pl.BlockSpec((tm, d), lambda i: (i, 0)),
            pl.BlockSpec((k, tm), lambda i: (0, i)),
            _resident((k, d)),
        ],
        out_specs=pl.BlockSpec((tm, d), lambda i: (i, 0)),
        out_shape=jax.ShapeDtypeStruct((n, d), F32),
        compiler_params=_params("parallel"),
        name="out_proj_t",
    )(h, a_t, w)


ONES_ROWS = 16
FLASH_LEAD = 2


def _flash_pipelined(ss_ref, n_past, diag_tile, score_fn, value_fn, hd, width):
    n_chain = ss_ref.shape[1]
    n_step = n_past + 1
    last = jnp.maximum(n_past - 1, 0)
    init = tuple((jnp.full((1, width), NEG, F32), jnp.zeros((hd + ONES_ROWS, width), F32)) for _ in range(n_chain))

    def step(i, carries, slot, prefetch):
        nxt = jnp.minimum(i, last)
        cur = jnp.where(i == 0, diag_tile, i - 1)
        if prefetch:
            for c in range(min(FLASH_LEAD, n_chain)):
                ss_ref[1 - slot, c] = score_fn(c, nxt)
        out = []
        for c in range(n_chain):
            s = ss_ref[slot, c]
            m, acc = carries[c]
            m_new = jnp.maximum(m, jnp.max(s, axis=0, keepdims=True))
            p = jnp.exp2(s - m_new).astype(BF16)
            out.append((m_new, jnp.exp2(m - m_new) * acc + _dot(value_fn(c, cur), p)))
            if prefetch and c + FLASH_LEAD < n_chain:
                ss_ref[1 - slot, c + FLASH_LEAD] = score_fn(c + FLASH_LEAD, nxt)
        return tuple(out)

    def two_steps(k, carries):
        return step(2 * k + 1, step(2 * k, carries, 0, True), 1, True)

    carries = lax.fori_loop(0, n_step // 2, two_steps, init)
    return lax.cond(n_step % 2 == 1, lambda c: step(n_step - 1, c, 0, False), lambda c: c, carries)


def _moba_kernel(qt_ref, kt_ref, vt_ref, ot_ref, ka_ref, ss_ref, *, seq, hb, hd):
    t = ATTN_TILE
    nb = seq // t
    kw = ka_ref.shape[-1]
    rows = [slice(h * hd, (h + 1) * hd) for h in range(hb)]
    col = lax.broadcasted_iota(jnp.int32, (t, kw - hd), 1)

    def fill_keys(j, _):
        j0 = pl.multiple_of(j * t, t)
        onehot = jnp.where(col == j, 1.0, 0.0).astype(BF16)
        for h in range(hb):
            ka_ref[h, pl.ds(j0, t), :] = jnp.concatenate([kt_ref[rows[h], pl.ds(j0, t)].T, onehot], axis=1)
        return 0

    lax.fori_loop(0, nb, fill_keys, 0)
    avg = jnp.where(lax.broadcasted_iota(jnp.int32, (nb, seq), 1) // t
                    == lax.broadcasted_iota(jnp.int32, (nb, seq), 0), 1.0 / t, 0.0).astype(BF16)
    km = []
    for h in range(hb):
        kmean = _dot(avg, ka_ref[h])
        km_hi = kmean.astype(BF16)
        km.append((km_hi, (kmean - km_hi.astype(F32)).astype(BF16)))
    blk = lax.broadcasted_iota(jnp.int32, (nb, t), 0)
    causal = (lax.broadcasted_iota(jnp.int32, (t, t), 0) <= lax.broadcasted_iota(jnp.int32, (t, t), 1))
    ones = jnp.ones((ONES_ROWS, t), BF16)

    def q_tile(qi, _):
        q0 = pl.multiple_of(qi * t, t)
        qs = [qt_ref[r, pl.ds(q0, t)] for r in rows]
        qz = [jnp.concatenate([q, jnp.zeros((kw - hd, t), BF16)], axis=0) for q in qs]
        gates = [_dot(km[h][0], qz[h]) + _dot(km[h][1], qz[h]) for h in range(hb)]
        for h in range(hb):
            ss_ref[0, h] = jnp.where(causal, _dot(ka_ref[h, pl.ds(q0, t), :], qz[h]), NEG)
        past = blk < qi
        qa = []
        for h in range(hb):
            gate = jnp.where(past, gates[h], NEG)
            sel = jnp.zeros((nb, t), jnp.bool_)
            for _ in range(min(MOBA_TOPK, nb)):
                top = jnp.max(gate, axis=0, keepdims=True)
                first = jnp.min(jnp.where(gate == top, blk, nb), axis=0, keepdims=True)
                hit = blk == first
                sel = jnp.logical_or(sel, hit)
                gate = jnp.where(hit, -jnp.inf, gate)
            bias = jnp.where(jnp.logical_and(sel, past), 0.0, NEG).astype(BF16)
            qa.append(jnp.concatenate([qs[h], bias, jnp.zeros((kw - hd - nb, t), BF16)], axis=0))

        def score(h, j):
            return _dot(ka_ref[h, pl.ds(pl.multiple_of(j * t, t), t), :], qa[h])

        def value(h, j):
            return jnp.concatenate([vt_ref[rows[h], pl.ds(pl.multiple_of(j * t, t), t)], ones], axis=0)

        carries = _flash_pipelined(ss_ref, qi, qi, score, value, hd, t)
        for h in range(hb):
            _, acc = carries[h]
            ot_ref[rows[h], pl.ds(q0, t)] = (acc[:hd] / acc[hd:hd + 1]).astype(ot_ref.dtype)
        return 0

    lax.fori_loop(0, nb, q_tile, 0)


MOBA_HEADS_PER_STEP = 8


def _moba_attention(qkv_t, *, batch, seq, heads, head_dim):
    n = batch * seq
    hb = MOBA_HEADS_PER_STEP
    hg = heads // hb
    blk = lambda third: pl.BlockSpec((hb * head_dim, seq), lambda b, h: (third * hg + h, b))
    return pl.pallas_call(
        functools.partial(_moba_kernel, seq=seq, hb=hb, hd=head_dim),
        grid=(batch, hg),
        in_specs=[blk(0), blk(1), blk(2)],
        out_specs=blk(0),
        out_shape=jax.ShapeDtypeStruct((heads * head_dim, n), BF16),
        scratch_shapes=[pltpu.VMEM((hb, seq, LANES), BF16), pltpu.VMEM((2, hb, ATTN_TILE, ATTN_TILE), F32)],
        compiler_params=_params("parallel", "parallel"),
        name="moba_attention",
    )(qkv_t, qkv_t, qkv_t)


def _moba_layer(h, g, w_in, q_gain, k_gain, w_out, *, batch, seq):
    heads = MOBA_HEADS
    hd = w_in.shape[1] // (3 * heads)
    hw = heads * hd
    assert seq % MOBA_BLOCK == 0 and MOBA_BLOCK == ATTN_TILE and hd + seq // MOBA_BLOCK <= LANES
    col = jnp.concatenate([jnp.tile(q_gain, heads) * (hd ** -0.5 * LOG2E), jnp.tile(k_gain, heads), jnp.ones((hw,), F32)])
    segs = ((0, hw, "norm"), (hw, 2 * hw, "norm"), (2 * hw, 3 * hw, "raw"))
    (qkv_t,) = _proj_t(h, g, w_in.T.astype(BF16), col.reshape(-1, 1), segs, head_dim=hd)
    o_t = _moba_attention(qkv_t, batch=batch, seq=seq, heads=heads, head_dim=hd)
    return _out_proj_t(h, o_t, w_out.astype(BF16))


def _lru_proj_kernel(x_ref, g_ref, w_ref, xb_ref, y_ref, *, width):
    xn = _rms_rows(x_ref[...], g_ref[...]).astype(BF16)
    xb_ref[...] = _dot(xn, w_ref[:, :width])
    y_ref[...] = _gelu_tanh(_dot(xn, w_ref[:, width:]))


def _lru_proj(h, g, w, *, tm=512):
    n, d = h.shape
    width = w.shape[1] // 2
    return pl.pallas_call(
        functools.partial(_lru_proj_kernel, width=width),
        grid=(n // tm,),
        in_specs=[pl.BlockSpec((tm, d), lambda i: (i, 0)), _resident((1, d)), _resident(w.shape)],
        out_specs=[pl.BlockSpec((tm, width), lambda i: (i, 0))] * 2,
        out_shape=[jax.ShapeDtypeStruct((n, width), F32)] * 2,
        compiler_params=_params("parallel"),
        name="lru_proj",
    )(h, g.reshape(1, d), w)


LRU_CONV_PAD = 8


def _lru_kernel(xb_ref, y_ref, cw_ref, cb_ref, wa_ref, ba_ref, wx_ref, bx_ref, lam_ref, o_ref,
                xpad, a_s, b_s, h_s, h_carry, *, tile, width):
    si = pl.program_id(1)
    pad = LRU_CONV_PAD

    @pl.when(si == 0)
    def _():
        xpad[0:pad, :] = jnp.zeros((pad, width), F32)
        h_carry[...] = jnp.zeros((1, width), F32)

    xpad[pad:pad + tile, :] = xb_ref[...]
    xc = cb_ref[...] + cw_ref[0:1, :] * xpad[pad - CONV_WIDTH + 1:pad - CONV_WIDTH + 1 + tile, :]
    for k in range(1, CONV_WIDTH):
        off = pad - CONV_WIDTH + 1 + k
        xc = xc + cw_ref[k:k + 1, :] * xpad[off:off + tile, :]
    xcb = xc.astype(BF16)
    first = (lax.broadcasted_iota(jnp.int32, (tile, 1), 0) + si * tile) == 0
    bw = width // LRU_BLOCKS
    for n in range(LRU_BLOCKS):
        sl = slice(n * bw, (n + 1) * bw)
        r = jax.nn.sigmoid(_dot(xcb[:, sl], wa_ref[n]) + ba_ref[:, sl])
        i = jax.nn.sigmoid(_dot(xcb[:, sl], wx_ref[n]) + bx_ref[:, sl])
        z = -lam_ref[:, sl]
        softplus = jnp.maximum(z, 0.0) + jnp.log1p(jnp.exp(-jnp.abs(z)))
        a = jnp.exp(-LRU_C * r * softplus)
        mult = jnp.where(first, 1.0, jnp.sqrt(1.0 - a * a))
        a_s[:, sl] = a
        b_s[:, sl] = mult * i * xc[:, sl]

    def step(t, h):
        h = a_s[pl.ds(t, 1), :] * h + b_s[pl.ds(t, 1), :]
        h_s[pl.ds(t, 1), :] = h
        return h

    h_carry[...] = lax.fori_loop(0, tile, step, h_carry[...], unroll=8)
    o_ref[...] = (h_s[...] * y_ref[...]).astype(o_ref.dtype)
    xpad[0:pad, :] = xpad[tile:tile + pad, :]


def _lru_recurrence(xb, y, cw, cb, wa, ba, wx, bx, lam, *, batch, seq, tile=512):
    n, width = xb.shape
    nt = seq // tile
    row = lambda b, s: (b * nt + s, 0)
    vec = _resident((1, width))
    return pl.pallas_call(
        functools.partial(_lru_kernel, tile=tile, width=width),
        grid=(batch, nt),
        in_specs=[
            pl.BlockSpec((tile, width), row),
            pl.BlockSpec((tile, width), row),
            _resident(cw.shape), vec,
            _resident(wa.shape), vec,
            _resident(wx.shape), vec,
            vec,
        ],
        out_specs=pl.BlockSpec((tile, width), row),
        out_shape=jax.ShapeDtypeStruct((n, width), BF16),
        scratch_shapes=[
            pltpu.VMEM((tile + LRU_CONV_PAD, width), F32),
            pltpu.VMEM((tile, width), F32),
            pltpu.VMEM((tile, width), F32),
            pltpu.VMEM((tile, width), F32),
            pltpu.VMEM((1, width), F32),
        ],
        compiler_params=_params("parallel", "arbitrary"),
        name="lru_recurrence",
    )(xb, y, cw, cb.reshape(1, width), wa, ba.reshape(1, width), wx, bx.reshape(1, width), lam.reshape(1, width))


def _lru_layer(h, g, w_in, conv_w, conv_b, wa, ba, wx, bx, lam, w_out, *, batch, seq):
    xb, y = _lru_proj(h, g, w_in.astype(BF16))
    gated = _lru_recurrence(xb, y, conv_w.reshape(CONV_WIDTH, -1), conv_b, wa.astype(BF16), ba, wx.astype(BF16), bx, lam,
                            batch=batch, seq=seq)
    return _out_proj(h, gated, w_out.astype(BF16))


def _nsa_compress_kernel(rk_ref, rv_ref, pk_ref, pv_ref, kw1_ref, kw2_ref, vw1_ref, vw2t_ref, kg_ref, ko_ref, vot_ref):
    def hidden(r_ref, pos_ref, w1_ref):
        r = r_ref[...]
        half = r.shape[1]
        rows = r.shape[0]
        lo = _dot(r, w1_ref[:half, :])
        hi = _dot(r, w1_ref[half:, :])
        hi_next = pltpu.roll(hi, shift=rows - 1, axis=0)
        pos = _dot(pos_ref[...], w1_ref[...])[0:1, :]
        return _gelu_tanh(lo + hi_next + pos).astype(BF16)

    k = _dot(hidden(rk_ref, pk_ref, kw1_ref), kw2_ref[...])
    ko_ref[...] = _rms_rows(k, kg_ref[...]).astype(ko_ref.dtype)
    vot_ref[...] = _dot_nt(vw2t_ref[...], hidden(rv_ref, pv_ref, vw1_ref)).astype(vot_ref.dtype)


def _nsa_compress(rk, rv, pos_k, pos_v, kw1, kw2, vw1, vw2t, k_gain):
    bg, rows, wide = rk.shape
    hd = kw2.shape[1]
    blk = pl.BlockSpec((None, rows, wide), lambda i: (i, 0, 0))
    return pl.pallas_call(
        _nsa_compress_kernel,
        grid=(bg,),
        in_specs=[blk, blk, _resident(pos_k.shape), _resident(pos_v.shape), _resident(kw1.shape), _resident(kw2.shape),
                  _resident(vw1.shape), _resident(vw2t.shape), _resident((1, hd))],
        out_specs=[pl.BlockSpec((None, rows, hd), lambda i: (i, 0, 0)), pl.BlockSpec((None, hd, rows), lambda i: (i, 0, 0))],
        out_shape=[jax.ShapeDtypeStruct((bg, rows, hd), BF16), jax.ShapeDtypeStruct((bg, hd, rows), BF16)],
        compiler_params=_params("parallel"),
        name="nsa_compress",
    )(rk, rv, pos_k, pos_v, kw1, kw2, vw1, vw2t, k_gain.reshape(1, hd))


def _nsa_kernel(qt_ref, kc_ref, vct_ref, kst_ref, vst_ref, kwt_ref, vwt_ref, gt_ref, ot_ref, ks_ref, kw_ref, ss_ref,
                *, seq):
    t = ATTN_TILE
    hd = NSA_HEAD_DIM
    ncr = seq // NSA_CMP_STRIDE
    nsl = seq // NSA_SEL_BLOCK
    n_top = min(NSA_TOPN, nsl)
    sel_col = lax.broadcasted_iota(jnp.int32, (t, nsl), 1)
    sel_blk = lax.broadcasted_iota(jnp.int32, (t, nsl), 0) // NSA_SEL_BLOCK

    def fill_keys(j, _):
        j0 = pl.multiple_of(j * t, t)
        onehot = jnp.where(sel_col == j * (t // NSA_SEL_BLOCK) + sel_blk, 1.0, 0.0).astype(BF16)
        ks_ref[pl.ds(j0, t), :] = jnp.concatenate([kst_ref[:, pl.ds(j0, t)].T, onehot], axis=1)
        kw_ref[pl.ds(j0, t), :] = kwt_ref[:, pl.ds(j0, t)].T
        return 0

    lax.fori_loop(0, seq // t, fill_keys, 0)
    gate_row0 = pl.program_id(1) * (NSA_HPG * NSA_N_BRANCH)
    jn = lax.broadcasted_iota(jnp.int32, (nsl, ncr), 0) * NSA_SEL_BLOCK
    cn = lax.broadcasted_iota(jnp.int32, (nsl, ncr), 1) * NSA_CMP_STRIDE
    ov_t = jnp.where(jnp.logical_and(cn < jn + NSA_SEL_BLOCK, cn + NSA_CMP_BLOCK > jn), 1.0, 0.0).astype(BF16)
    tw = NSA_HPG * t
    diff = lax.broadcasted_iota(jnp.int32, (t, tw), 0) - (lax.broadcasted_iota(jnp.int32, (t, tw), 1) & (t - 1))
    causal = (lax.broadcasted_iota(jnp.int32, (t, t), 0) <= lax.broadcasted_iota(jnp.int32, (t, t), 1))
    cmp_end = lax.broadcasted_iota(jnp.int32, (ncr, tw), 0) * NSA_CMP_STRIDE + (NSA_CMP_BLOCK - 1)
    cmp_q = lax.broadcasted_iota(jnp.int32, (ncr, tw), 1) & (t - 1)
    blk_j = lax.broadcasted_iota(jnp.int32, (nsl, t), 0)
    ones = jnp.ones((ONES_ROWS, t), BF16)

    def q_tile(qi, _):
        q0 = pl.multiple_of(qi * t, t)
        qs = [qt_ref[hp * hd:(hp + 1) * hd, pl.ds(q0, t)] for hp in range(NSA_HPG)]
        q = jnp.concatenate(qs, axis=1)

        k1 = pl.multiple_of(jnp.maximum(qi - 1, 0) * t, t)
        k2 = pl.multiple_of(jnp.maximum(qi - 2, 0) * t, t)
        s_cmp = _dot(kc_ref[...], q)
        s_win = [_dot(kw_ref[pl.ds(k0, t), :], q) for k0 in (q0, k1, k2)]

        m_c = cmp_end <= q0 + cmp_q
        s = jnp.where(m_c, s_cmp, NEG)
        e = jnp.where(m_c, jnp.exp2(s - jnp.max(s, axis=0, keepdims=True)), 0.0)
        den = jnp.sum(e, axis=0, keepdims=True)
        p = e / jnp.where(den > 0.0, den, 1.0)
        o_cmp = _dot(vct_ref[...], p.astype(BF16))
        p_sum = p[:, 0:t]
        for hp in range(1, NSA_HPG):
            p_sum = p_sum + p[:, hp * t:(hp + 1) * t]
        ps_hi = p_sum.astype(BF16)
        ps_lo = (p_sum - ps_hi.astype(F32)).astype(BF16)
        imp = _dot(ov_t, ps_hi) + _dot(ov_t, ps_lo)
        qpos = q0 + lax.broadcasted_iota(jnp.int32, (nsl, t), 1)
        back = qpos // NSA_SEL_BLOCK - blk_j
        forced = jnp.logical_or(blk_j == 0, jnp.logical_and(back >= 0, back < NSA_LOCAL_BLOCKS))
        valid = blk_j * NSA_SEL_BLOCK <= qpos
        score = jnp.where(valid, jnp.where(forced, FORCE, imp), NEG)
        n_grp = nsl // SUBLANES
        tiles = [score[a * SUBLANES:(a + 1) * SUBLANES] for a in range(n_grp)]
        sub = lax.broadcasted_iota(jnp.int32, (SUBLANES, t), 0)

        def count_group(mt, ranks):
            ranks = list(ranks)
            for r in range(SUBLANES):
                row = tiles[mt][r:r + 1, :]
                for a in range(n_grp):
                    if a < mt:
                        ahead = jnp.where(row > tiles[a], 1, 0)
                    elif a > mt:
                        ahead = jnp.where(row >= tiles[a], 1, 0)
                    else:
                        ahead = jnp.where(sub > r, jnp.where(row >= tiles[a], 1, 0), jnp.where(row > tiles[a], 1, 0))
                    ranks[a] = ranks[a] + ahead
            return tuple(ranks)

        ranks = tuple(jnp.zeros((SUBLANES, t), jnp.int32) for _ in range(n_grp))
        last_grp = (q0 + t - 1) // (NSA_SEL_BLOCK * SUBLANES)
        for mt in range(n_grp):
            ranks = lax.cond(mt <= last_grp, functools.partial(count_group, mt), lambda r: r, ranks)
        sel = jnp.logical_and(jnp.concatenate(ranks, axis=0) < n_top, valid)
        bias = jnp.where(sel, 0.0, NEG).astype(BF16)

        lo1 = jnp.where(qi >= 1, -t, t)
        lo2 = jnp.where(qi >= 2, 0, t)
        s_win = [jnp.where(diff <= 0, s_win[0], NEG), jnp.where(diff > lo1, s_win[1], NEG),
                 jnp.where(diff > lo2, s_win[2], NEG)]
        m_w = functools.reduce(jnp.maximum, [jnp.max(s, axis=0, keepdims=True) for s in s_win])
        acc_w = functools.reduce(jnp.add, [
            _dot(jnp.concatenate([vwt_ref[:, pl.ds(k0, t)], ones], axis=0), jnp.exp2(s - m_w).astype(BF16))
            for s, k0 in zip(s_win, (q0, k1, k2))])
        o_win = acc_w[:hd] / acc_w[hd:hd + 1]

        qa = [jnp.concatenate([qh, bias], axis=0) for qh in qs]
        for hp in range(NSA_HPG):
            ss_ref[0, hp] = jnp.where(causal, _dot(ks_ref[pl.ds(q0, t), :], qa[hp]), NEG)

        def score(hp, j):
            return _dot(ks_ref[pl.ds(pl.multiple_of(j * t, t), t), :], qa[hp])

        def value(hp, j):
            return jnp.concatenate([vst_ref[:, pl.ds(pl.multiple_of(j * t, t), t)], ones], axis=0)

        carries = _flash_pipelined(ss_ref, qi, qi, score, value, hd, t)
        for hp in range(NSA_HPG):
            _, acc = carries[hp]
            cols = slice(hp * t, (hp + 1) * t)
            gate = [gt_ref[pl.ds(gate_row0 + NSA_N_BRANCH * hp + br, 1), pl.ds(q0, t)] for br in range(NSA_N_BRANCH)]
            o = gate[0] * o_cmp[:, cols] + gate[1] * (acc[:hd] / acc[hd:hd + 1]) + gate[2] * o_win[:, cols]
            ot_ref[hp * hd:(hp + 1) * hd, pl.ds(q0, t)] = o.astype(ot_ref.dtype)
        return 0

    lax.fori_loop(0, seq // t, q_tile, 0)


def _nsa_attention(main_t, k_cmp, v_cmp_t, gates_t, *, batch, seq, row_ks, row_vs, row_kw, row_vw):
    groups, hd, hpg = NSA_KV_GROUPS, NSA_HEAD_DIM, NSA_HPG
    n = batch * seq
    ncr = seq // NSA_CMP_STRIDE
    nsl = seq // NSA_SEL_BLOCK
    gh = hpg * hd
    kv_rows = lambda row0: pl.BlockSpec((hd, seq), lambda b, g: (row0 // hd + g, b))
    return pl.pallas_call(
        functools.partial(_nsa_kernel, seq=seq),
        grid=(batch, groups),
        in_specs=[
            pl.BlockSpec((gh, seq), lambda b, g: (g, b)),
            pl.BlockSpec((None, ncr, hd), lambda b, g: (b * groups + g, 0, 0)),
            pl.BlockSpec((None, hd, ncr), lambda b, g: (b * groups + g, 0, 0)),
            kv_rows(row_ks), kv_rows(row_vs), kv_rows(row_kw), kv_rows(row_vw),
            pl.BlockSpec((gates_t.shape[0], seq), lambda b, g: (0, b)),
        ],
        out_specs=pl.BlockSpec((gh, seq), lambda b, g: (g, b)),
        out_shape=jax.ShapeDtypeStruct((groups * gh, n), BF16),
        scratch_shapes=[pltpu.VMEM((seq, hd + nsl), BF16), pltpu.VMEM((seq, hd), BF16),
                        pltpu.VMEM((2, hpg, ATTN_TILE, ATTN_TILE), F32)],
        compiler_params=_params("parallel", "parallel"),
        name="nsa_attention",
    )(main_t, k_cmp, v_cmp_t, main_t, main_t, main_t, main_t, gates_t)


def _nsa_layer(h, g, w_in, gate_b, q_gain, kc_gain, ks_gain, kw_gain, pos_k, pos_v, ck_w1, ck_w2, cv_w1, cv_w2, w_out,
               *, batch, seq):
    heads, groups, hd, hpg = NSA_HEADS, NSA_KV_GROUPS, NSA_HEAD_DIM, NSA_HPG
    assert seq % ATTN_TILE == 0 and NSA_WINDOW == 2 * ATTN_TILE and NSA_CMP_BLOCK == 2 * NSA_CMP_STRIDE
    qw, kvw = heads * hd, groups * hd
    r_kc, r_vc, r_ks, r_vs, r_kw, r_vw, r_g = (qw + i * kvw for i in range(7))
    n_gate = NSA_N_BRANCH * heads
    ones = jnp.ones((kvw,), F32)
    col = jnp.concatenate([jnp.tile(q_gain, heads) * (hd ** -0.5 * LOG2E), ones, ones, jnp.tile(ks_gain, groups), ones,
                           jnp.tile(kw_gain, groups), ones, gate_b])
    segs = ((0, qw, "norm"), (r_kc, r_ks, "raw"), (r_ks, r_vs, "norm"), (r_vs, r_kw, "raw"), (r_kw, r_vw, "norm"),
            (r_vw, r_g, "raw"), (r_g, r_g + n_gate, "sigmoid"))
    main_t, gates_t = _proj_t(h, g, w_in.T.astype(BF16), col.reshape(-1, 1), segs, head_dim=hd)

    def token_major(r0):
        return main_t[r0:r0 + kvw].reshape(groups, hd, batch, seq).transpose(2, 0, 3, 1)

    stride = NSA_CMP_STRIDE
    rk = token_major(r_kc).reshape(batch * groups, seq // stride, stride * hd)
    rv = token_major(r_vc).reshape(batch * groups, seq // stride, stride * hd)

    def pos_rows(p):
        return jnp.broadcast_to(p.reshape(1, -1), (8, p.size)).astype(BF16)

    k_cmp, v_cmp_t = _nsa_compress(rk, rv, pos_rows(pos_k), pos_rows(pos_v), ck_w1.astype(BF16), ck_w2.astype(BF16),
                                   cv_w1.astype(BF16), cv_w2.T.astype(BF16), kc_gain)
    o_t = _nsa_attention(main_t, k_cmp, v_cmp_t, gates_t, batch=batch, seq=seq,
                         row_ks=r_ks, row_vs=r_vs, row_kw=r_kw, row_vw=r_vw)
    return _out_proj_t(h, o_t, w_out.astype(BF16))


def kernel(x, norm_g, ffn1_wi, ffn1_wo, ffn2_wi, ffn2_wo, moba_w_in, moba_q_gain, moba_k_gain, moba_w_out, lru_w_in, lru_conv_w, lru_conv_b, lru_wa, lru_ba, lru_wx, lru_bx, lru_lam, lru_w_out, nsa_w_in, nsa_gate_b, nsa_q_gain, nsa_kc_gain, nsa_ks_gain, nsa_kw_gain, nsa_pos_k, nsa_pos_v, nsa_ck_w1, nsa_ck_w2, nsa_cv_w1, nsa_cv_w2, nsa_w_out):
    batch, seq, d = x.shape
    n_mixers = 3
    h = x.reshape(batch * seq, d)
    for i in range(norm_g.shape[0]):
        j, kind = divmod(i, n_mixers)
        h = _ffn(h, norm_g[i, 0], ffn1_wi[i].astype(BF16), ffn1_wo[i].astype(BF16))
        if kind == 0:
            h = _moba_layer(h, norm_g[i, 1], moba_w_in[j], moba_q_gain[j], moba_k_gain[j], moba_w_out[j],
                            batch=batch, seq=seq)
        elif kind == 1:
            h = _lru_layer(h, norm_g[i, 1], lru_w_in[j], lru_conv_w[j], lru_conv_b[j], lru_wa[j], lru_ba[j], lru_wx[j],
                           lru_bx[j], lru_lam[j], lru_w_out[j], batch=batch, seq=seq)
        else:
            h = _nsa_layer(h, norm_g[i, 1], nsa_w_in[j], nsa_gate_b[j], nsa_q_gain[j], nsa_kc_gain[j], nsa_ks_gain[j],
                           nsa_kw_gain[j], nsa_pos_k[j], nsa_pos_v[j], nsa_ck_w1[j], nsa_ck_w2[j], nsa_cv_w1[j],
                           nsa_cv_w2[j], nsa_w_out[j], batch=batch, seq=seq)
        h = _ffn(h, norm_g[i, 2], ffn2_wi[i].astype(BF16), ffn2_wo[i].astype(BF16))
    return h.reshape(batch, seq, d)
```

```python
import functools

import jax
import jax.numpy as jnp
from jax import lax
from jax.experimental import pallas as pl
from jax.experimental.pallas import tpu as pltpu

F32 = jnp.float32
BF16 = jnp.bfloat16

EPS = 1e-6
NEG = -1e30
FORCE = 1e30
FFN_RESID = 0.5

MOBA_HEADS = 16
MOBA_BLOCK = 256
MOBA_TOPK = 3

LRU_BLOCKS = 4
CONV_WIDTH = 4
LRU_C = 8.0

NSA_HEADS = 16
NSA_KV_GROUPS = 4
NSA_HEAD_DIM = 64
NSA_HPG = NSA_HEADS // NSA_KV_GROUPS
NSA_CMP_BLOCK = 32
NSA_CMP_STRIDE = 16
NSA_SEL_BLOCK = 64
NSA_TOPN = 16
NSA_LOCAL_BLOCKS = 2
NSA_WINDOW = 512
NSA_N_BRANCH = 3

V7X_VMEM_BYTES = 64 * 1024 * 1024
VMEM_LIMIT_BYTES = V7X_VMEM_BYTES - 8 * 1024 * 1024

LANES = 128
SUBLANES = 8
LOG2E = 1.4426950408889634

ATTN_TILE = 256


def _params(*sem):
    return pltpu.CompilerParams(dimension_semantics=sem, vmem_limit_bytes=VMEM_LIMIT_BYTES)


def _resident(shape):
    nd = len(shape)
    return pl.BlockSpec(shape, lambda *_: (0,) * nd, pipeline_mode=pl.Buffered(1))


def _rms_rows(x, g):
    return x * lax.rsqrt(jnp.mean(x * x, axis=-1, keepdims=True) + EPS) * g


def _gelu_tanh(x):
    c = 0.7978845608028654
    return 0.5 * x * (1.0 + jnp.tanh(c * (x + 0.044715 * (x * x * x))))


def _dot(a, b):
    return jnp.dot(a, b, preferred_element_type=F32)


def _dot_nt(a, b):
    return lax.dot_general(a, b, (((1,), (1,)), ((), ())), preferred_element_type=F32)


def _ffn_kernel(x_ref, *refs, d_ff, chunk, mix):
    if mix is None:
        g_ref, wi_ref, wo_ref, o_ref = refs
        x = x_ref[...]
    else:
        a_ref, wm_ref, g_ref, wi_ref, wo_ref, o_ref = refs
        contract = (((0,), (0,)), ((), ())) if mix == "feature_major" else (((1,), (0,)), ((), ()))
        x = x_ref[...] + lax.dot_general(a_ref[...], wm_ref[...], contract, preferred_element_type=F32)
    xn = _rms_rows(x, g_ref[...]).astype(BF16)
    acc = jnp.zeros(x.shape, F32)
    for c0 in range(0, d_ff, chunk):
        gate = _dot(xn, wi_ref[:, c0:c0 + chunk].astype(BF16))
        up = _dot(xn, wi_ref[:, d_ff + c0:d_ff + c0 + chunk].astype(BF16))
        act = (gate * jax.nn.sigmoid(gate) * up).astype(BF16)
        acc = acc + _dot(act, wo_ref[c0:c0 + chunk, :].astype(BF16))
    o_ref[...] = x + FFN_RESID * acc


def _ffn(h, g, wi, wo, *, mixed=None, tm=512, chunk=256):
    n, d = h.shape
    d_ff = wo.shape[0]
    row = pl.BlockSpec((tm, d), lambda i: (i, 0))
    operands, specs, mix = [h], [row], None
    if mixed is not None:
        a, w_mix, mix = mixed
        k = w_mix.shape[0]
        a_spec = pl.BlockSpec((k, tm), lambda i: (0, i)) if mix == "feature_major" else pl.BlockSpec((tm, k), lambda i: (i, 0))
        operands += [a, w_mix]
        specs += [a_spec, _resident(w_mix.shape)]
    return pl.pallas_call(
        functools.partial(_ffn_kernel, d_ff=d_ff, chunk=chunk, mix=mix),
        grid=(n // tm,),
        in_specs=specs + [_resident((1, d)), _resident(wi.shape), _resident(wo.shape)],
        out_specs=row,
        out_shape=jax.ShapeDtypeStruct((n, d), F32),
        compiler_params=_params("parallel"),
        name="ffn",
    )(*operands, g.reshape(1, d), wi, wo)


def _proj_t_kernel(x_ref, g_ref, wt_ref, col_ref, o_ref, *aux_refs, segs, head_dim):
    xn = _rms_rows(x_ref[...], g_ref[...]).astype(BF16)
    tm = xn.shape[0]
    for r0, r1, mode in segs:
        acc = _dot_nt(wt_ref[r0:r1, :], xn)
        if mode == "raw":
            o_ref[r0:r1, :] = acc.astype(o_ref.dtype)
        elif mode == "norm":
            nh = (r1 - r0) // head_dim
            a3 = acc.reshape(nh, head_dim, tm)
            ms = jnp.mean(a3 * a3, axis=1, keepdims=True)
            y = a3 * lax.rsqrt(ms + EPS) * col_ref[r0:r1, :].reshape(nh, head_dim, 1)
            o_ref[r0:r1, :] = y.reshape(r1 - r0, tm).astype(o_ref.dtype)
        else:
            (aux_ref,) = aux_refs
            aux_ref[...] = jax.nn.sigmoid(acc + col_ref[r0:r1, :])


def _proj_t(h, g, wt, col, segs, *, head_dim, tm=512):
    n, d = h.shape
    rows = wt.shape[0]
    n_main = max(r1 for _, r1, mode in segs if mode != "sigmoid")
    aux = [(r0, r1) for r0, r1, mode in segs if mode == "sigmoid"]
    out_shape = [jax.ShapeDtypeStruct((n_main, n), BF16)]
    out_specs = [pl.BlockSpec((n_main, tm), lambda i: (0, i))]
    if aux:
        ((a0, a1),) = aux
        out_shape.append(jax.ShapeDtypeStruct((a1 - a0, n), F32))
        out_specs.append(pl.BlockSpec((a1 - a0, tm), lambda i: (0, i)))
    return pl.pallas_call(
        functools.partial(_proj_t_kernel, segs=segs, head_dim=head_dim),
        grid=(n // tm,),
        in_specs=[
            pl.BlockSpec((tm, d), lambda i: (i, 0)),
            _resident((1, d)),
            _resident((rows, d)),
            _resident((rows, 1)),
        ],
        out_specs=out_specs,
        out_shape=out_shape,
        compiler_params=_params("parallel"),
        name="proj_t",
    )(h, g.reshape(1, d), wt, col)


ONES_ROWS = 16
FLASH_LEAD = 3


def _flash_pipelined(ss_ref, n_past, diag_tile, score_fn, value_fn, hd, width):
    n_chain = ss_ref.shape[1]
    n_step = n_past + 1
    last = jnp.maximum(n_past - 1, 0)
    init = tuple((jnp.full((1, width), NEG, F32), jnp.zeros((hd + ONES_ROWS, width), F32)) for _ in range(n_chain))

    def step(i, carries, slot, prefetch):
        nxt = jnp.minimum(i, last)
        cur = jnp.where(i == 0, diag_tile, i - 1)
        if prefetch:
            for c in range(min(FLASH_LEAD, n_chain)):
                ss_ref[1 - slot, c] = score_fn(c, nxt)
        out = []
        for c in range(n_chain):
            s = ss_ref[slot, c]
            m, acc = carries[c]
            m_new = jnp.maximum(m, jnp.max(s, axis=0, keepdims=True))
            p = jnp.exp2(s - m_new).astype(BF16)
            out.append((m_new, jnp.exp2(m - m_new) * acc + _dot(value_fn(c, cur), p)))
            if prefetch and c + FLASH_LEAD < n_chain:
                ss_ref[1 - slot, c + FLASH_LEAD] = score_fn(c + FLASH_LEAD, nxt)
        return tuple(out)

    def two_steps(k, carries):
        return step(2 * k + 1, step(2 * k, carries, 0, True), 1, True)

    carries = lax.fori_loop(0, n_step // 2, two_steps, init)
    return lax.cond(n_step % 2 == 1, lambda c: step(n_step - 1, c, 0, False), lambda c: c, carries)


def _moba_kernel(qt_ref, kt_ref, vt_ref, ot_ref, ka_ref, ss_ref, *, seq, hb, hd):
    t = ATTN_TILE
    nb = seq // t
    kw = ka_ref.shape[-1]
    rows = [slice(h * hd, (h + 1) * hd) for h in range(hb)]
    col = lax.broadcasted_iota(jnp.int32, (t, kw - hd), 1)

    def fill_keys(j, _):
        j0 = pl.multiple_of(j * t, t)
        onehot = jnp.where(col == j, 1.0, 0.0).astype(BF16)
        for h in range(hb):
            ka_ref[h, pl.ds(j0, t), :] = jnp.concatenate([kt_ref[rows[h], pl.ds(j0, t)].T, onehot], axis=1)
        return 0

    lax.fori_loop(0, nb, fill_keys, 0)
    avg = jnp.where(lax.broadcasted_iota(jnp.int32, (nb, seq), 1) // t
                    == lax.broadcasted_iota(jnp.int32, (nb, seq), 0), 1.0 / t, 0.0).astype(BF16)
    km = []
    for h in range(hb):
        kmean = _dot(avg, ka_ref[h])
        km_hi = kmean.astype(BF16)
        km.append((km_hi, (kmean - km_hi.astype(F32)).astype(BF16)))
    blk = lax.broadcasted_iota(jnp.int32, (nb, t), 0)
    causal = (lax.broadcasted_iota(jnp.int32, (t, t), 0) <= lax.broadcasted_iota(jnp.int32, (t, t), 1))
    ones = jnp.ones((ONES_ROWS, t), BF16)

    def q_tile(qi, _):
        q0 = pl.multiple_of(qi * t, t)
        qs = [qt_ref[r, pl.ds(q0, t)] for r in rows]
        qz = [jnp.concatenate([q, jnp.zeros((kw - hd, t), BF16)], axis=0) for q in qs]
        gates = [_dot(km[h][0], qz[h]) + _dot(km[h][1], qz[h]) for h in range(hb)]
        for h in range(hb):
            ss_ref[0, h] = jnp.where(causal, _dot(ka_ref[h, pl.ds(q0, t), :], qz[h]), NEG)
        past = blk < qi
        qa = []
        for h in range(hb):
            gate = jnp.where(past, gates[h], NEG)
            sel = jnp.zeros((nb, t), jnp.bool_)
            for _ in range(min(MOBA_TOPK, nb)):
                top = jnp.max(gate, axis=0, keepdims=True)
                first = jnp.min(jnp.where(gate == top, blk, nb), axis=0, keepdims=True)
                hit = blk == first
                sel = jnp.logical_or(sel, hit)
                gate = jnp.where(hit, -jnp.inf, gate)
            bias = jnp.where(jnp.logical_and(sel, past), 0.0, NEG).astype(BF16)
            qa.append(jnp.concatenate([qs[h], bias, jnp.zeros((kw - hd - nb, t), BF16)], axis=0))

        def score(h, j):
            return _dot(ka_ref[h, pl.ds(pl.multiple_of(j * t, t), t), :], qa[h])

        def value(h, j):
            return jnp.concatenate([vt_ref[rows[h], pl.ds(pl.multiple_of(j * t, t), t)], ones], axis=0)

        carries = _flash_pipelined(ss_ref, qi, qi, score, value, hd, t)
        for h in range(hb):
            _, acc = carries[h]
            ot_ref[rows[h], pl.ds(q0, t)] = (acc[:hd] / acc[hd:hd + 1]).astype(ot_ref.dtype)
        return 0

    lax.fori_loop(0, nb, q_tile, 0)


MOBA_HEADS_PER_STEP = 8


def _moba_attention(qkv_t, *, batch, seq, heads, head_dim):
    n = batch * seq
    hb = MOBA_HEADS_PER_STEP
    hg = heads // hb
    blk = lambda third: pl.BlockSpec((hb * head_dim, seq), lambda b, h: (third * hg + h, b))
    return pl.pallas_call(
        functools.partial(_moba_kernel, seq=seq, hb=hb, hd=head_dim),
        grid=(batch, hg),
        in_specs=[blk(0), blk(1), blk(2)],
        out_specs=blk(0),
        out_shape=jax.ShapeDtypeStruct((heads * head_dim, n), BF16),
        scratch_shapes=[pltpu.VMEM((hb, seq, LANES), BF16), pltpu.VMEM((2, hb, ATTN_TILE, ATTN_TILE), F32)],
        compiler_params=_params("parallel", "parallel"),
        name="moba_attention",
    )(qkv_t, qkv_t, qkv_t)


def _moba_layer(h, g, w_in, q_gain, k_gain, w_out, *, batch, seq):
    heads = MOBA_HEADS
    hd = w_in.shape[1] // (3 * heads)
    hw = heads * hd
    assert seq % MOBA_BLOCK == 0 and MOBA_BLOCK == ATTN_TILE and hd + seq // MOBA_BLOCK <= LANES
    col = jnp.concatenate([jnp.tile(q_gain, heads) * (hd ** -0.5 * LOG2E), jnp.tile(k_gain, heads), jnp.ones((hw,), F32)])
    segs = ((0, hw, "norm"), (hw, 2 * hw, "norm"), (2 * hw, 3 * hw, "raw"))
    (qkv_t,) = _proj_t(h, g, w_in.T.astype(BF16), col.reshape(-1, 1), segs, head_dim=hd)
    o_t = _moba_attention(qkv_t, batch=batch, seq=seq, heads=heads, head_dim=hd)
    return o_t, w_out.astype(BF16), "feature_major"


def _lru_proj_kernel(x_ref, g_ref, w_ref, xb_ref, y_ref, *, width):
    xn = _rms_rows(x_ref[...], g_ref[...]).astype(BF16)
    xb_ref[...] = _dot(xn, w_ref[:, :width])
    y_ref[...] = _gelu_tanh(_dot(xn, w_ref[:, width:]))


def _lru_proj(h, g, w, *, tm=512):
    n, d = h.shape
    width = w.shape[1] // 2
    return pl.pallas_call(
        functools.partial(_lru_proj_kernel, width=width),
        grid=(n // tm,),
        in_specs=[pl.BlockSpec((tm, d), lambda i: (i, 0)), _resident((1, d)), _resident(w.shape)],
        out_specs=[pl.BlockSpec((tm, width), lambda i: (i, 0))] * 2,
        out_shape=[jax.ShapeDtypeStruct((n, width), F32)] * 2,
        compiler_params=_params("parallel"),
        name="lru_proj",
    )(h, g.reshape(1, d), w)


LRU_CONV_PAD = 8


def _lru_kernel(xb_ref, y_ref, cw_ref, cb_ref, wa_ref, ba_ref, wx_ref, bx_ref, lam_ref, o_ref,
                xpad, a_s, b_s, h_s, h_carry, *, tile, width):
    si = pl.program_id(1)
    pad = LRU_CONV_PAD

    @pl.when(si == 0)
    def _():
        xpad[0:pad, :] = jnp.zeros((pad, width), F32)
        h_carry[...] = jnp.zeros((1, width), F32)

    xpad[pad:pad + tile, :] = xb_ref[...]
    xc = cb_ref[...] + cw_ref[0:1, :] * xpad[pad - CONV_WIDTH + 1:pad - CONV_WIDTH + 1 + tile, :]
    for k in range(1, CONV_WIDTH):
        off = pad - CONV_WIDTH + 1 + k
        xc = xc + cw_ref[k:k + 1, :] * xpad[off:off + tile, :]
    xcb = xc.astype(BF16)
    first = (lax.broadcasted_iota(jnp.int32, (tile, 1), 0) + si * tile) == 0
    bw = width // LRU_BLOCKS
    for n in range(LRU_BLOCKS):
        sl = slice(n * bw, (n + 1) * bw)
        r = jax.nn.sigmoid(_dot(xcb[:, sl], wa_ref[n]) + ba_ref[:, sl])
        i = jax.nn.sigmoid(_dot(xcb[:, sl], wx_ref[n]) + bx_ref[:, sl])
        z = -lam_ref[:, sl]
        softplus = jnp.maximum(z, 0.0) + jnp.log1p(jnp.exp(-jnp.abs(z)))
        a = jnp.exp(-LRU_C * r * softplus)
        mult = jnp.where(first, 1.0, jnp.sqrt(1.0 - a * a))
        a_s[:, sl] = a
        b_s[:, sl] = mult * i * xc[:, sl]

    def step(t, h):
        h = a_s[pl.ds(t, 1), :] * h + b_s[pl.ds(t, 1), :]
        h_s[pl.ds(t, 1), :] = h
        return h

    h_carry[...] = lax.fori_loop(0, tile, step, h_carry[...], unroll=8)
    o_ref[...] = (h_s[...] * y_ref[...]).astype(o_ref.dtype)
    xpad[0:pad, :] = xpad[tile:tile + pad, :]


def _lru_recurrence(xb, y, cw, cb, wa, ba, wx, bx, lam, *, batch, seq, tile=512):
    n, width = xb.shape
    nt = seq // tile
    row = lambda b, s: (b * nt + s, 0)
    vec = _resident((1, width))
    return pl.pallas_call(
        functools.partial(_lru_kernel, tile=tile, width=width),
        grid=(batch, nt),
        in_specs=[
            pl.BlockSpec((tile, width), row),
            pl.BlockSpec((tile, width), row),
            _resident(cw.shape), vec,
            _resident(wa.shape), vec,
            _resident(wx.shape), vec,
            vec,
        ],
        out_specs=pl.BlockSpec((tile, width), row),
        out_shape=jax.ShapeDtypeStruct((n, width), BF16),
        scratch_shapes=[
            pltpu.VMEM((tile + LRU_CONV_PAD, width), F32),
            pltpu.VMEM((tile, width), F32),
            pltpu.VMEM((tile, width), F32),
            pltpu.VMEM((tile, width), F32),
            pltpu.VMEM((1, width), F32),
        ],
        compiler_params=_params("parallel", "arbitrary"),
        name="lru_recurrence",
    )(xb, y, cw, cb.reshape(1, width), wa, ba.reshape(1, width), wx, bx.reshape(1, width), lam.reshape(1, width))


def _lru_layer(h, g, w_in, conv_w, conv_b, wa, ba, wx, bx, lam, w_out, *, batch, seq):
    xb, y = _lru_proj(h, g, w_in.astype(BF16))
    gated = _lru_recurrence(xb, y, conv_w.reshape(CONV_WIDTH, -1), conv_b, wa.astype(BF16), ba, wx.astype(BF16), bx, lam,
                            batch=batch, seq=seq)
    return gated, w_out.astype(BF16), "token_major"


def _nsa_compress_kernel(rk_ref, rv_ref, pk_ref, pv_ref, kw1_ref, kw2_ref, vw1_ref, vw2t_ref, kg_ref, ko_ref, vot_ref):
    def hidden(r_ref, pos_ref, w1_ref):
        r = r_ref[...]
        half = r.shape[1]
        rows = r.shape[0]
        lo = _dot(r, w1_ref[:half, :])
        hi = _dot(r, w1_ref[half:, :])
        hi_next = pltpu.roll(hi, shift=rows - 1, axis=0)
        pos = _dot(pos_ref[...], w1_ref[...])[0:1, :]
        return _gelu_tanh(lo + hi_next + pos).astype(BF16)

    k = _dot(hidden(rk_ref, pk_ref, kw1_ref), kw2_ref[...])
    ko_ref[...] = _rms_rows(k, kg_ref[...]).astype(ko_ref.dtype)
    vot_ref[...] = _dot_nt(vw2t_ref[...], hidden(rv_ref, pv_ref, vw1_ref)).astype(vot_ref.dtype)


def _nsa_compress(rk, rv, pos_k, pos_v, kw1, kw2, vw1, vw2t, k_gain):
    bg, rows, wide = rk.shape
    hd = kw2.shape[1]
    blk = pl.BlockSpec((None, rows, wide), lambda i: (i, 0, 0))
    return pl.pallas_call(
        _nsa_compress_kernel,
        grid=(bg,),
        in_specs=[blk, blk, _resident(pos_k.shape), _resident(pos_v.shape), _resident(kw1.shape), _resident(kw2.shape),
                  _resident(vw1.shape), _resident(vw2t.shape), _resident((1, hd))],
        out_specs=[pl.BlockSpec((None, rows, hd), lambda i: (i, 0, 0)), pl.BlockSpec((None, hd, rows), lambda i: (i, 0, 0))],
        out_shape=[jax.ShapeDtypeStruct((bg, rows, hd), BF16), jax.ShapeDtypeStruct((bg, hd, rows), BF16)],
        compiler_params=_params("parallel"),
        name="nsa_compress",
    )(rk, rv, pos_k, pos_v, kw1, kw2, vw1, vw2t, k_gain.reshape(1, hd))


def _nsa_kernel(qt_ref, kc_ref, vct_ref, kst_ref, vst_ref, kwt_ref, vwt_ref, gt_ref, ot_ref, ks_ref, kw_ref, ss_ref,
                *, seq):
    t = ATTN_TILE
    hd = NSA_HEAD_DIM
    ncr = seq // NSA_CMP_STRIDE
    nsl = seq // NSA_SEL_BLOCK
    n_top = min(NSA_TOPN, nsl)
    sel_col = lax.broadcasted_iota(jnp.int32, (t, nsl), 1)
    sel_blk = lax.broadcasted_iota(jnp.int32, (t, nsl), 0) // NSA_SEL_BLOCK

    def fill_keys(j, _):
        j0 = pl.multiple_of(j * t, t)
        onehot = jnp.where(sel_col == j * (t // NSA_SEL_BLOCK) + sel_blk, 1.0, 0.0).astype(BF16)
        ks_ref[pl.ds(j0, t), :] = jnp.concatenate([kst_ref[:, pl.ds(j0, t)].T, onehot], axis=1)
        kw_ref[pl.ds(j0, t), :] = kwt_ref[:, pl.ds(j0, t)].T
        return 0

    lax.fori_loop(0, seq // t, fill_keys, 0)
    gate_row0 = pl.program_id(1) * (NSA_HPG * NSA_N_BRANCH)
    jn = lax.broadcasted_iota(jnp.int32, (nsl, ncr), 0) * NSA_SEL_BLOCK
    cn = lax.broadcasted_iota(jnp.int32, (nsl, ncr), 1) * NSA_CMP_STRIDE
    ov_t = jnp.where(jnp.logical_and(cn < jn + NSA_SEL_BLOCK, cn + NSA_CMP_BLOCK > jn), 1.0, 0.0).astype(BF16)
    tw = NSA_HPG * t
    diff = lax.broadcasted_iota(jnp.int32, (t, tw), 0) - (lax.broadcasted_iota(jnp.int32, (t, tw), 1) & (t - 1))
    causal = (lax.broadcasted_iota(jnp.int32, (t, t), 0) <= lax.broadcasted_iota(jnp.int32, (t, t), 1))
    cmp_end = lax.broadcasted_iota(jnp.int32, (ncr, tw), 0) * NSA_CMP_STRIDE + (NSA_CMP_BLOCK - 1)
    cmp_q = lax.broadcasted_iota(jnp.int32, (ncr, tw), 1) & (t - 1)
    blk_j = lax.broadcasted_iota(jnp.int32, (nsl, t), 0)
    ones = jnp.ones((ONES_ROWS, t), BF16)

    def q_tile(qi, _):
        q0 = pl.multiple_of(qi * t, t)
        qs = [qt_ref[hp * hd:(hp + 1) * hd, pl.ds(q0, t)] for hp in range(NSA_HPG)]
        q = jnp.concatenate(qs, axis=1)

        k1 = pl.multiple_of(jnp.maximum(qi - 1, 0) * t, t)
        k2 = pl.multiple_of(jnp.maximum(qi - 2, 0) * t, t)
        s_cmp = _dot(kc_ref[...], q)
        s_win = [_dot(kw_ref[pl.ds(k0, t), :], q) for k0 in (q0, k1, k2)]

        s = jnp.where(cmp_end <= q0 + cmp_q, s_cmp, NEG)
        e = jnp.exp2(s - jnp.max(s, axis=0, keepdims=True))
        den = jnp.sum(e, axis=0, keepdims=True)
        p = e * jnp.where(q0 + cmp_q[0:1, :] >= NSA_CMP_BLOCK - 1, 1.0 / den, 0.0)
        o_cmp = _dot(vct_ref[...], p.astype(BF16))
        p_sum = p[:, 0:t]
        for hp in range(1, NSA_HPG):
            p_sum = p_sum + p[:, hp * t:(hp + 1) * t]
        ps_hi = p_sum.astype(BF16)
        ps_lo = (p_sum - ps_hi.astype(F32)).astype(BF16)
        imp = _dot(ov_t, ps_hi) + _dot(ov_t, ps_lo)
        qpos = q0 + lax.broadcasted_iota(jnp.int32, (nsl, t), 1)
        back = qpos // NSA_SEL_BLOCK - blk_j
        forced = jnp.logical_or(blk_j == 0, jnp.logical_and(back >= 0, back < NSA_LOCAL_BLOCKS))
        valid = blk_j * NSA_SEL_BLOCK <= qpos
        score = jnp.where(valid, jnp.where(forced, FORCE, imp), NEG)
        n_grp = nsl // SUBLANES
        tiles = [score[a * SUBLANES:(a + 1) * SUBLANES] for a in range(n_grp)]
        sub = lax.broadcasted_iota(jnp.int32, (SUBLANES, t), 0)

        def count_group(mt, ranks):
            ranks = list(ranks)
            for r in range(SUBLANES):
                row = tiles[mt][r:r + 1, :]
                for a in range(n_grp):
                    if a < mt:
                        ahead = jnp.where(row > tiles[a], 1, 0)
                    elif a > mt:
                        ahead = jnp.where(row >= tiles[a], 1, 0)
                    else:
                        ahead = jnp.where(sub > r, jnp.where(row >= tiles[a], 1, 0), jnp.where(row > tiles[a], 1, 0))
                    ranks[a] = ranks[a] + ahead
            return tuple(ranks)

        ranks = tuple(jnp.zeros((SUBLANES, t), jnp.int32) for _ in range(n_grp))
        last_grp = (q0 + t - 1) // (NSA_SEL_BLOCK * SUBLANES)
        for mt in range(n_grp):
            ranks = lax.cond(mt <= last_grp, functools.partial(count_group, mt), lambda r: r, ranks)
        sel = jnp.logical_and(jnp.concatenate(ranks, axis=0) < n_top, valid)
        bias = jnp.where(sel, 0.0, NEG).astype(BF16)

        lo1 = jnp.where(qi >= 1, -t, t)
        lo2 = jnp.where(qi >= 2, 0, t)
        s_win = [jnp.where(diff <= 0, s_win[0], NEG), jnp.where(diff > lo1, s_win[1], NEG),
                 jnp.where(diff > lo2, s_win[2], NEG)]
        m_w = functools.reduce(jnp.maximum, [jnp.max(s, axis=0, keepdims=True) for s in s_win])
        acc_w = functools.reduce(jnp.add, [
            _dot(jnp.concatenate([vwt_ref[:, pl.ds(k0, t)], ones], axis=0), jnp.exp2(s - m_w).astype(BF16))
            for s, k0 in zip(s_win, (q0, k1, k2))])
        o_win = acc_w[:hd] / acc_w[hd:hd + 1]

        qa = [jnp.concatenate([qh, bias], axis=0) for qh in qs]
        for hp in range(NSA_HPG):
            ss_ref[0, hp] = jnp.where(causal, _dot(ks_ref[pl.ds(q0, t), :], qa[hp]), NEG)

        def score(hp, j):
            return _dot(ks_ref[pl.ds(pl.multiple_of(j * t, t), t), :], qa[hp])

        def value(hp, j):
            return jnp.concatenate([vst_ref[:, pl.ds(pl.multiple_of(j * t, t), t)], ones], axis=0)

        carries = _flash_pipelined(ss_ref, qi, qi, score, value, hd, t)
        for hp in range(NSA_HPG):
            _, acc = carries[hp]
            cols = slice(hp * t, (hp + 1) * t)
            gate = [gt_ref[pl.ds(gate_row0 + NSA_N_BRANCH * hp + br, 1), pl.ds(q0, t)] for br in range(NSA_N_BRANCH)]
            o = gate[0] * o_cmp[:, cols] + gate[1] * (acc[:hd] / acc[hd:hd + 1]) + gate[2] * o_win[:, cols]
            ot_ref[hp * hd:(hp + 1) * hd, pl.ds(q0, t)] = o.astype(ot_ref.dtype)
        return 0

    lax.fori_loop(0, seq // t, q_tile, 0)


def _nsa_attention(main_t, k_cmp, v_cmp_t, gates_t, *, batch, seq, row_ks, row_vs, row_kw, row_vw):
    groups, hd, hpg = NSA_KV_GROUPS, NSA_HEAD_DIM, NSA_HPG
    n = batch * seq
    ncr = seq // NSA_CMP_STRIDE
    nsl = seq // NSA_SEL_BLOCK
    gh = hpg * hd
    kv_rows = lambda row0: pl.BlockSpec((hd, seq), lambda b, g: (row0 // hd + g, b))
    return pl.pallas_call(
        functools.partial(_nsa_kernel, seq=seq),
        grid=(batch, groups),
        in_specs=[
            pl.BlockSpec((gh, seq), lambda b, g: (g, b)),
            pl.BlockSpec((None, ncr, hd), lambda b, g: (b * groups + g, 0, 0)),
            pl.BlockSpec((None, hd, ncr), lambda b, g: (b * groups + g, 0, 0)),
            kv_rows(row_ks), kv_rows(row_vs), kv_rows(row_kw), kv_rows(row_vw),
            pl.BlockSpec((gates_t.shape[0], seq), lambda b, g: (0, b)),
        ],
        out_specs=pl.BlockSpec((gh, seq), lambda b, g: (g, b)),
        out_shape=jax.ShapeDtypeStruct((groups * gh, n), BF16),
        scratch_shapes=[pltpu.VMEM((seq, hd + nsl), BF16), pltpu.VMEM((seq, hd), BF16),
                        pltpu.VMEM((2, hpg, ATTN_TILE, ATTN_TILE), F32)],
        compiler_params=_params("parallel", "parallel"),
        name="nsa_attention",
    )(main_t, k_cmp, v_cmp_t, main_t, main_t, main_t, main_t, gates_t)


def _nsa_layer(h, g, w_in, gate_b, q_gain, kc_gain, ks_gain, kw_gain, pos_k, pos_v, ck_w1, ck_w2, cv_w1, cv_w2, w_out,
               *, batch, seq):
    heads, groups, hd, hpg = NSA_HEADS, NSA_KV_GROUPS, NSA_HEAD_DIM, NSA_HPG
    assert seq % ATTN_TILE == 0 and NSA_WINDOW == 2 * ATTN_TILE and NSA_CMP_BLOCK == 2 * NSA_CMP_STRIDE
    qw, kvw = heads * hd, groups * hd
    r_kc, r_vc, r_ks, r_vs, r_kw, r_vw, r_g = (qw + i * kvw for i in range(7))
    n_gate = NSA_N_BRANCH * heads
    ones = jnp.ones((kvw,), F32)
    col = jnp.concatenate([jnp.tile(q_gain, heads) * (hd ** -0.5 * LOG2E), ones, ones, jnp.tile(ks_gain, groups), ones,
                           jnp.tile(kw_gain, groups), ones, gate_b])
    segs = ((0, qw, "norm"), (r_kc, r_ks, "raw"), (r_ks, r_vs, "norm"), (r_vs, r_kw, "raw"), (r_kw, r_vw, "norm"),
            (r_vw, r_g, "raw"), (r_g, r_g + n_gate, "sigmoid"))
    main_t, gates_t = _proj_t(h, g, w_in.T.astype(BF16), col.reshape(-1, 1), segs, head_dim=hd)

    def token_major(r0):
        return main_t[r0:r0 + kvw].reshape(groups, hd, batch, seq).transpose(2, 0, 3, 1)

    stride = NSA_CMP_STRIDE
    rk = token_major(r_kc).reshape(batch * groups, seq // stride, stride * hd)
    rv = token_major(r_vc).reshape(batch * groups, seq // stride, stride * hd)

    def pos_rows(p):
        return jnp.broadcast_to(p.reshape(1, -1), (8, p.size)).astype(BF16)

    k_cmp, v_cmp_t = _nsa_compress(rk, rv, pos_rows(pos_k), pos_rows(pos_v), ck_w1.astype(BF16), ck_w2.astype(BF16),
                                   cv_w1.astype(BF16), cv_w2.T.astype(BF16), kc_gain)
    o_t = _nsa_attention(main_t, k_cmp, v_cmp_t, gates_t, batch=batch, seq=seq,
                         row_ks=r_ks, row_vs=r_vs, row_kw=r_kw, row_vw=r_vw)
    return o_t, w_out.astype(BF16), "feature_major"


def kernel(x, norm_g, ffn1_wi, ffn1_wo, ffn2_wi, ffn2_wo, moba_w_in, moba_q_gain, moba_k_gain, moba_w_out, lru_w_in, lru_conv_w, lru_conv_b, lru_wa, lru_ba, lru_wx, lru_bx, lru_lam, lru_w_out, nsa_w_in, nsa_gate_b, nsa_q_gain, nsa_kc_gain, nsa_ks_gain, nsa_kw_gain, nsa_pos_k, nsa_pos_v, nsa_ck_w1, nsa_ck_w2, nsa_cv_w1, nsa_cv_w2, nsa_w_out):
    batch, seq, d = x.shape
    n_mixers = 3
    h = x.reshape(batch * seq, d)
    for i in range(norm_g.shape[0]):
        j, kind = divmod(i, n_mixers)
        h = _ffn(h, norm_g[i, 0], ffn1_wi[i], ffn1_wo[i])
        if kind == 0:
            mixed = _moba_layer(h, norm_g[i, 1], moba_w_in[j], moba_q_gain[j], moba_k_gain[j], moba_w_out[j],
                                batch=batch, seq=seq)
        elif kind == 1:
            mixed = _lru_layer(h, norm_g[i, 1], lru_w_in[j], lru_conv_w[j], lru_conv_b[j], lru_wa[j], lru_ba[j],
                               lru_wx[j], lru_bx[j], lru_lam[j], lru_w_out[j], batch=batch, seq=seq)
        else:
            mixed = _nsa_layer(h, norm_g[i, 1], nsa_w_in[j], nsa_gate_b[j], nsa_q_gain[j], nsa_kc_gain[j],
                               nsa_ks_gain[j], nsa_kw_gain[j], nsa_pos_k[j], nsa_pos_v[j], nsa_ck_w1[j], nsa_ck_w2[j],
                               nsa_cv_w1[j], nsa_cv_w2[j], nsa_w_out[j], batch=batch, seq=seq)
        h = _ffn(h, norm_g[i, 2], ffn2_wi[i], ffn2_wo[i], mixed=mixed)
    return h.reshape(batch, seq, d)
```

```python
import functools

import jax
import jax.numpy as jnp
from jax import lax
from jax.experimental import pallas as pl
from jax.experimental.pallas import tpu as pltpu

F32 = jnp.float32
BF16 = jnp.bfloat16

EPS = 1e-6
NEG = -1e30
FORCE = 1e30
FFN_RESID = 0.5

MOBA_HEADS = 16
MOBA_BLOCK = 256
MOBA_TOPK = 3

LRU_BLOCKS = 4
CONV_WIDTH = 4
LRU_C = 8.0

NSA_HEADS = 16
NSA_KV_GROUPS = 4
NSA_HEAD_DIM = 64
NSA_HPG = NSA_HEADS // NSA_KV_GROUPS
NSA_CMP_BLOCK = 32
NSA_CMP_STRIDE = 16
NSA_SEL_BLOCK = 64
NSA_TOPN = 16
NSA_LOCAL_BLOCKS = 2
NSA_WINDOW = 512
NSA_N_BRANCH = 3

V7X_VMEM_BYTES = 64 * 1024 * 1024
VMEM_LIMIT_BYTES = V7X_VMEM_BYTES - 8 * 1024 * 1024

LANES = 128
SUBLANES = 8
LOG2E = 1.4426950408889634

ATTN_TILE = 256


def _params(*sem):
    return pltpu.CompilerParams(dimension_semantics=sem, vmem_limit_bytes=VMEM_LIMIT_BYTES)


def _resident(shape):
    nd = len(shape)
    return pl.BlockSpec(shape, lambda *_: (0,) * nd, pipeline_mode=pl.Buffered(1))


def _layer_block(stack, layer):
    nd = stack.ndim
    return pl.BlockSpec((None,) + stack.shape[1:], lambda *_: (layer,) + (0,) * (nd - 1), pipeline_mode=pl.Buffered(1))


def _rms_rows(x, g):
    return x * lax.rsqrt(jnp.mean(x * x, axis=-1, keepdims=True) + EPS) * g


def _gelu_tanh(x):
    c = 0.7978845608028654
    return 0.5 * x * (1.0 + jnp.tanh(c * (x + 0.044715 * (x * x * x))))


def _dot(a, b):
    return jnp.dot(a, b, preferred_element_type=F32)


def _dot_nt(a, b):
    return lax.dot_general(a, b, (((1,), (1,)), ((), ())), preferred_element_type=F32)


def _ffn_kernel(x_ref, *refs, d_ff, chunk, mix):
    if mix is None:
        g_ref, wi_ref, wo_ref, o_ref = refs
        x = x_ref[...]
    else:
        a_ref, wm_ref, g_ref, wi_ref, wo_ref, o_ref = refs
        contract = (((0,), (0,)), ((), ())) if mix == "feature_major" else (((1,), (0,)), ((), ()))
        x = x_ref[...] + lax.dot_general(a_ref[...], wm_ref[...].astype(BF16), contract, preferred_element_type=F32)
    xn = _rms_rows(x, g_ref[...]).astype(BF16)
    acc = jnp.zeros(x.shape, F32)
    for c0 in range(0, d_ff, chunk):
        gate = _dot(xn, wi_ref[:, c0:c0 + chunk].astype(BF16))
        up = _dot(xn, wi_ref[:, d_ff + c0:d_ff + c0 + chunk].astype(BF16))
        act = (gate * jax.nn.sigmoid(gate) * up).astype(BF16)
        acc = acc + _dot(act, wo_ref[c0:c0 + chunk, :].astype(BF16))
    o_ref[...] = x + FFN_RESID * acc


def _ffn(h, g, wi, wo, layer, *, mixed=None, tm=512, chunk=256):
    n, d = h.shape
    d_ff = wo.shape[1]
    row = pl.BlockSpec((tm, d), lambda i: (i, 0))
    operands, specs, mix = [h], [row], None
    if mixed is not None:
        a, w_mix, index, mix = mixed
        k = w_mix.shape[1]
        a_spec = pl.BlockSpec((k, tm), lambda i: (0, i)) if mix == "feature_major" else pl.BlockSpec((tm, k), lambda i: (i, 0))
        operands += [a, w_mix]
        specs += [a_spec, _layer_block(w_mix, index)]
    return pl.pallas_call(
        functools.partial(_ffn_kernel, d_ff=d_ff, chunk=chunk, mix=mix),
        grid=(n // tm,),
        in_specs=specs + [_resident((1, d)), _layer_block(wi, layer), _layer_block(wo, layer)],
        out_specs=row,
        out_shape=jax.ShapeDtypeStruct((n, d), F32),
        compiler_params=_params("parallel"),
        name="ffn",
    )(*operands, g.reshape(1, d), wi, wo)


PROJ_XPOSE_COLS = 512


def _proj_t_kernel(x_ref, g_ref, w_ref, col_ref, *refs, segs, head_dim, n_main):
    if len(refs) == 2:
        (o_ref, wt_ref), tail_ref, aux_ref = refs, None, None
    else:
        tail_ref, o_ref, aux_ref, wt_ref = refs

    @pl.when(pl.program_id(0) == 0)
    def _():
        for c0 in range(0, n_main, PROJ_XPOSE_COLS):
            wt_ref[c0:c0 + PROJ_XPOSE_COLS, :] = w_ref[:, c0:c0 + PROJ_XPOSE_COLS].T.astype(BF16)

    xn = _rms_rows(x_ref[...], g_ref[...]).astype(BF16)
    tm = xn.shape[0]
    for r0, r1, mode in segs:
        acc = _dot_nt(tail_ref[...] if mode == "sigmoid" else wt_ref[r0:r1, :], xn)
        if mode == "raw":
            o_ref[r0:r1, :] = acc.astype(o_ref.dtype)
        elif mode == "norm":
            nh = (r1 - r0) // head_dim
            a3 = acc.reshape(nh, head_dim, tm)
            ms = jnp.mean(a3 * a3, axis=1, keepdims=True)
            y = a3 * lax.rsqrt(ms + EPS) * col_ref[r0:r1, :].reshape(nh, head_dim, 1)
            o_ref[r0:r1, :] = y.reshape(r1 - r0, tm).astype(o_ref.dtype)
        else:
            aux_ref[...] = jax.nn.sigmoid(acc + col_ref[r0:r1, :])


def _proj_t(h, g, w, layer, col, segs, *, head_dim, tm=512):
    n, d = h.shape
    rows = w.shape[2]
    n_main = max(r1 for _, r1, mode in segs if mode != "sigmoid")
    aux = [(r0, r1) for r0, r1, mode in segs if mode == "sigmoid"]
    assert n_main % PROJ_XPOSE_COLS == 0
    operands = [h, g.reshape(1, d), w, col]
    in_specs = [pl.BlockSpec((tm, d), lambda i: (i, 0)), _resident((1, d)), _layer_block(w, layer), _resident((rows, 1))]
    out_shape = [jax.ShapeDtypeStruct((n_main, n), BF16)]
    out_specs = [pl.BlockSpec((n_main, tm), lambda i: (0, i))]
    if aux:
        ((a0, a1),) = aux
        operands.append(w[layer, :, a0:a1].T.astype(BF16))
        in_specs.append(_resident((a1 - a0, d)))
        out_shape.append(jax.ShapeDtypeStruct((a1 - a0, n), F32))
        out_specs.append(pl.BlockSpec((a1 - a0, tm), lambda i: (0, i)))
    return pl.pallas_call(
        functools.partial(_proj_t_kernel, segs=segs, head_dim=head_dim, n_main=n_main),
        grid=(n // tm,),
        in_specs=in_specs,
        out_specs=out_specs,
        out_shape=out_shape,
        scratch_shapes=[pltpu.VMEM((n_main, d), BF16)],
        compiler_params=_params("arbitrary"),
        name="proj_t",
    )(*operands)


ONES_ROWS = 16
FLASH_LEAD = 3


def _flash_pipelined(ss_ref, n_past, diag_tile, score_fn, value_fn, hd, width):
    n_chain = ss_ref.shape[1]
    n_step = n_past + 1
    last = jnp.maximum(n_past - 1, 0)
    init = tuple((jnp.full((1, width), NEG, F32), jnp.zeros((hd + ONES_ROWS, width), F32)) for _ in range(n_chain))

    def step(i, carries, slot, prefetch):
        nxt = jnp.minimum(i, last)
        cur = jnp.where(i == 0, diag_tile, i - 1)
        if prefetch:
            for c in range(min(FLASH_LEAD, n_chain)):
                ss_ref[1 - slot, c] = score_fn(c, nxt)
        out = []
        for c in range(n_chain):
            s = ss_ref[slot, c]
            m, acc = carries[c]
            m_new = jnp.maximum(m, jnp.max(s, axis=0, keepdims=True))
            p = jnp.exp2(s - m_new).astype(BF16)
            out.append((m_new, jnp.exp2(m - m_new) * acc + _dot(value_fn(c, cur), p)))
            if prefetch and c + FLASH_LEAD < n_chain:
                ss_ref[1 - slot, c + FLASH_LEAD] = score_fn(c + FLASH_LEAD, nxt)
        return tuple(out)

    def two_steps(k, carries):
        return step(2 * k + 1, step(2 * k, carries, 0, True), 1, True)

    carries = lax.fori_loop(0, n_step // 2, two_steps, init)
    return lax.cond(n_step % 2 == 1, lambda c: step(n_step - 1, c, 0, False), lambda c: c, carries)


def _moba_kernel(qt_ref, kt_ref, vt_ref, ot_ref, ka_ref, ss_ref, *, seq, hb, hd):
    t = ATTN_TILE
    nb = seq // t
    kw = ka_ref.shape[-1]
    rows = [slice(h * hd, (h + 1) * hd) for h in range(hb)]
    col = lax.broadcasted_iota(jnp.int32, (t, kw - hd), 1)

    def fill_keys(j, _):
        j0 = pl.multiple_of(j * t, t)
        onehot = jnp.where(col == j, 1.0, 0.0).astype(BF16)
        for h in range(hb):
            ka_ref[h, pl.ds(j0, t), :] = jnp.concatenate([kt_ref[rows[h], pl.ds(j0, t)].T, onehot], axis=1)
        return 0

    lax.fori_loop(0, nb, fill_keys, 0)
    avg = jnp.where(lax.broadcasted_iota(jnp.int32, (nb, seq), 1) // t
                    == lax.broadcasted_iota(jnp.int32, (nb, seq), 0), 1.0 / t, 0.0).astype(BF16)
    km = []
    for h in range(hb):
        kmean = _dot(avg, ka_ref[h])
        km_hi = kmean.astype(BF16)
        km.append((km_hi, (kmean - km_hi.astype(F32)).astype(BF16)))
    blk = lax.broadcasted_iota(jnp.int32, (nb, t), 0)
    causal = (lax.broadcasted_iota(jnp.int32, (t, t), 0) <= lax.broadcasted_iota(jnp.int32, (t, t), 1))
    ones = jnp.ones((ONES_ROWS, t), BF16)

    def q_tile(qi, _):
        q0 = pl.multiple_of(qi * t, t)
        qs = [qt_ref[r, pl.ds(q0, t)] for r in rows]
        qz = [jnp.concatenate([q, jnp.zeros((kw - hd, t), BF16)], axis=0) for q in qs]
        gates = [_dot(km[h][0], qz[h]) + _dot(km[h][1], qz[h]) for h in range(hb)]
        for h in range(hb):
            ss_ref[0, h] = jnp.where(causal, _dot(ka_ref[h, pl.ds(q0, t), :], qz[h]), NEG)
        past = blk < qi
        qa = []
        for h in range(hb):
            gate = jnp.where(past, gates[h], NEG)
            sel = jnp.zeros((nb, t), jnp.bool_)
            for _ in range(min(MOBA_TOPK, nb)):
                top = jnp.max(gate, axis=0, keepdims=True)
                first = jnp.min(jnp.where(gate == top, blk, nb), axis=0, keepdims=True)
                hit = blk == first
                sel = jnp.logical_or(sel, hit)
                gate = jnp.where(hit, -jnp.inf, gate)
            bias = jnp.where(jnp.logical_and(sel, past), 0.0, NEG).astype(BF16)
            qa.append(jnp.concatenate([qs[h], bias, jnp.zeros((kw - hd - nb, t), BF16)], axis=0))

        def score(h, j):
            return _dot(ka_ref[h, pl.ds(pl.multiple_of(j * t, t), t), :], qa[h])

        def value(h, j):
            return jnp.concatenate([vt_ref[rows[h], pl.ds(pl.multiple_of(j * t, t), t)], ones], axis=0)

        carries = _flash_pipelined(ss_ref, qi, qi, score, value, hd, t)
        for h in range(hb):
            _, acc = carries[h]
            ot_ref[rows[h], pl.ds(q0, t)] = (acc[:hd] / acc[hd:hd + 1]).astype(ot_ref.dtype)
        return 0

    lax.fori_loop(0, nb, q_tile, 0)


MOBA_HEADS_PER_STEP = 8


def _moba_attention(qkv_t, *, batch, seq, heads, head_dim):
    n = batch * seq
    hb = MOBA_HEADS_PER_STEP
    hg = heads // hb
    blk = lambda third: pl.BlockSpec((hb * head_dim, seq), lambda b, h: (third * hg + h, b))
    return pl.pallas_call(
        functools.partial(_moba_kernel, seq=seq, hb=hb, hd=head_dim),
        grid=(batch, hg),
        in_specs=[blk(0), blk(1), blk(2)],
        out_specs=blk(0),
        out_shape=jax.ShapeDtypeStruct((heads * head_dim, n), BF16),
        scratch_shapes=[pltpu.VMEM((hb, seq, LANES), BF16), pltpu.VMEM((2, hb, ATTN_TILE, ATTN_TILE), F32)],
        compiler_params=_params("parallel", "parallel"),
        name="moba_attention",
    )(qkv_t, qkv_t, qkv_t)


def _moba_layer(h, g, w_in, j, q_gain, k_gain, w_out, *, batch, seq):
    heads = MOBA_HEADS
    hd = w_in.shape[2] // (3 * heads)
    hw = heads * hd
    assert seq % MOBA_BLOCK == 0 and MOBA_BLOCK == ATTN_TILE and hd + seq // MOBA_BLOCK <= LANES
    col = jnp.concatenate([jnp.tile(q_gain, heads) * (hd ** -0.5 * LOG2E), jnp.tile(k_gain, heads), jnp.ones((hw,), F32)])
    segs = ((0, hw, "norm"), (hw, 2 * hw, "norm"), (2 * hw, 3 * hw, "raw"))
    (qkv_t,) = _proj_t(h, g, w_in, j, col.reshape(-1, 1), segs, head_dim=hd)
    o_t = _moba_attention(qkv_t, batch=batch, seq=seq, heads=heads, head_dim=hd)
    return o_t, w_out, j, "feature_major"


def _lru_proj_kernel(x_ref, g_ref, w_ref, xb_ref, y_ref, wb_ref, *, width):
    @pl.when(pl.program_id(0) == 0)
    def _():
        wb_ref[...] = w_ref[...].astype(BF16)

    xn = _rms_rows(x_ref[...], g_ref[...]).astype(BF16)
    xb_ref[...] = _dot(xn, wb_ref[:, :width])
    y_ref[...] = _gelu_tanh(_dot(xn, wb_ref[:, width:]))


def _lru_proj(h, g, w, layer, *, tm=512):
    n, d = h.shape
    width = w.shape[2] // 2
    return pl.pallas_call(
        functools.partial(_lru_proj_kernel, width=width),
        grid=(n // tm,),
        in_specs=[pl.BlockSpec((tm, d), lambda i: (i, 0)), _resident((1, d)), _layer_block(w, layer)],
        out_specs=[pl.BlockSpec((tm, width), lambda i: (i, 0))] * 2,
        out_shape=[jax.ShapeDtypeStruct((n, width), F32)] * 2,
        scratch_shapes=[pltpu.VMEM(w.shape[1:], BF16)],
        compiler_params=_params("arbitrary"),
        name="lru_proj",
    )(h, g.reshape(1, d), w)


LRU_CONV_PAD = 8


def _lru_kernel(xb_ref, y_ref, cw_ref, cb_ref, wa_ref, ba_ref, wx_ref, bx_ref, lam_ref, o_ref,
                xpad, a_s, b_s, h_s, h_carry, *, tile, width):
    si = pl.program_id(1)
    pad = LRU_CONV_PAD

    @pl.when(si == 0)
    def _():
        xpad[0:pad, :] = jnp.zeros((pad, width), F32)
        h_carry[...] = jnp.zeros((1, width), F32)

    xpad[pad:pad + tile, :] = xb_ref[...]
    xc = cb_ref[...] + cw_ref[0:1, :] * xpad[pad - CONV_WIDTH + 1:pad - CONV_WIDTH + 1 + tile, :]
    for k in range(1, CONV_WIDTH):
        off = pad - CONV_WIDTH + 1 + k
        xc = xc + cw_ref[k:k + 1, :] * xpad[off:off + tile, :]
    xcb = xc.astype(BF16)
    first = (lax.broadcasted_iota(jnp.int32, (tile, 1), 0) + si * tile) == 0
    bw = width // LRU_BLOCKS
    for n in range(LRU_BLOCKS):
        sl = slice(n * bw, (n + 1) * bw)
        r = jax.nn.sigmoid(_dot(xcb[:, sl], wa_ref[n]) + ba_ref[:, sl])
        i = jax.nn.sigmoid(_dot(xcb[:, sl], wx_ref[n]) + bx_ref[:, sl])
        z = -lam_ref[:, sl]
        softplus = jnp.maximum(z, 0.0) + jnp.log1p(jnp.exp(-jnp.abs(z)))
        a = jnp.exp(-LRU_C * r * softplus)
        mult = jnp.where(first, 1.0, jnp.sqrt(1.0 - a * a))
        a_s[:, sl] = a
        b_s[:, sl] = mult * i * xc[:, sl]

    def step(t, h):
        h = a_s[pl.ds(t, 1), :] * h + b_s[pl.ds(t, 1), :]
        h_s[pl.ds(t, 1), :] = h
        return h

    h_carry[...] = lax.fori_loop(0, tile, step, h_carry[...], unroll=8)
    o_ref[...] = (h_s[...] * y_ref[...]).astype(o_ref.dtype)
    xpad[0:pad, :] = xpad[tile:tile + pad, :]


def _lru_recurrence(xb, y, cw, cb, wa, ba, wx, bx, lam, *, batch, seq, tile=512):
    n, width = xb.shape
    nt = seq // tile
    row = lambda b, s: (b * nt + s, 0)
    vec = _resident((1, width))
    return pl.pallas_call(
        functools.partial(_lru_kernel, tile=tile, width=width),
        grid=(batch, nt),
        in_specs=[
            pl.BlockSpec((tile, width), row),
            pl.BlockSpec((tile, width), row),
            _resident(cw.shape), vec,
            _resident(wa.shape), vec,
            _resident(wx.shape), vec,
            vec,
        ],
        out_specs=pl.BlockSpec((tile, width), row),
        out_shape=jax.ShapeDtypeStruct((n, width), BF16),
        scratch_shapes=[
            pltpu.VMEM((tile + LRU_CONV_PAD, width), F32),
            pltpu.VMEM((tile, width), F32),
            pltpu.VMEM((tile, width), F32),
            pltpu.VMEM((tile, width), F32),
            pltpu.VMEM((1, width), F32),
        ],
        compiler_params=_params("parallel", "arbitrary"),
        name="lru_recurrence",
    )(xb, y, cw, cb.reshape(1, width), wa, ba.reshape(1, width), wx, bx.reshape(1, width), lam.reshape(1, width))


def _lru_layer(h, g, w_in, j, conv_w, conv_b, wa, ba, wx, bx, lam, w_out, *, batch, seq):
    xb, y = _lru_proj(h, g, w_in, j)
    gated = _lru_recurrence(xb, y, conv_w.reshape(CONV_WIDTH, -1), conv_b, wa.astype(BF16), ba, wx.astype(BF16), bx, lam,
                            batch=batch, seq=seq)
    return gated, w_out, j, "token_major"


def _nsa_compress_kernel(rk_ref, rv_ref, pk_ref, pv_ref, kw1_ref, kw2_ref, vw1_ref, vw2t_ref, kg_ref, ko_ref, vot_ref):
    def hidden(r_ref, pos_ref, w1_ref):
        r = r_ref[...]
        half = r.shape[1]
        rows = r.shape[0]
        lo = _dot(r, w1_ref[:half, :])
        hi = _dot(r, w1_ref[half:, :])
        hi_next = pltpu.roll(hi, shift=rows - 1, axis=0)
        pos = _dot(pos_ref[...], w1_ref[...])[0:1, :]
        return _gelu_tanh(lo + hi_next + pos).astype(BF16)

    k = _dot(hidden(rk_ref, pk_ref, kw1_ref), kw2_ref[...])
    ko_ref[...] = _rms_rows(k, kg_ref[...]).astype(ko_ref.dtype)
    vot_ref[...] = _dot_nt(vw2t_ref[...], hidden(rv_ref, pv_ref, vw1_ref)).astype(vot_ref.dtype)


def _nsa_compress(rk, rv, pos_k, pos_v, kw1, kw2, vw1, vw2t, k_gain):
    bg, rows, wide = rk.shape
    hd = kw2.shape[1]
    blk = pl.BlockSpec((None, rows, wide), lambda i: (i, 0, 0))
    return pl.pallas_call(
        _nsa_compress_kernel,
        grid=(bg,),
        in_specs=[blk, blk, _resident(pos_k.shape), _resident(pos_v.shape), _resident(kw1.shape), _resident(kw2.shape),
                  _resident(vw1.shape), _resident(vw2t.shape), _resident((1, hd))],
        out_specs=[pl.BlockSpec((None, rows, hd), lambda i: (i, 0, 0)), pl.BlockSpec((None, hd, rows), lambda i: (i, 0, 0))],
        out_shape=[jax.ShapeDtypeStruct((bg, rows, hd), BF16), jax.ShapeDtypeStruct((bg, hd, rows), BF16)],
        compiler_params=_params("parallel"),
        name="nsa_compress",
    )(rk, rv, pos_k, pos_v, kw1, kw2, vw1, vw2t, k_gain.reshape(1, hd))


def _nsa_kernel(qt_ref, kc_ref, vct_ref, kst_ref, vst_ref, kwt_ref, vwt_ref, gt_ref, ot_ref, ks_ref, kw_ref, ss_ref,
                *, seq):
    t = ATTN_TILE
    hd = NSA_HEAD_DIM
    ncr = seq // NSA_CMP_STRIDE
    nsl = seq // NSA_SEL_BLOCK
    n_top = min(NSA_TOPN, nsl)
    sel_col = lax.broadcasted_iota(jnp.int32, (t, nsl), 1)
    sel_blk = lax.broadcasted_iota(jnp.int32, (t, nsl), 0) // NSA_SEL_BLOCK

    def fill_keys(j, _):
        j0 = pl.multiple_of(j * t, t)
        onehot = jnp.where(sel_col == j * (t // NSA_SEL_BLOCK) + sel_blk, 1.0, 0.0).astype(BF16)
        ks_ref[pl.ds(j0, t), :] = jnp.concatenate([kst_ref[:, pl.ds(j0, t)].T, onehot], axis=1)
        kw_ref[pl.ds(j0, t), :] = kwt_ref[:, pl.ds(j0, t)].T
        return 0

    lax.fori_loop(0, seq // t, fill_keys, 0)
    gate_row0 = pl.program_id(1) * (NSA_HPG * NSA_N_BRANCH)
    jn = lax.broadcasted_iota(jnp.int32, (nsl, ncr), 0) * NSA_SEL_BLOCK
    cn = lax.broadcasted_iota(jnp.int32, (nsl, ncr), 1) * NSA_CMP_STRIDE
    ov_t = jnp.where(jnp.logical_and(cn < jn + NSA_SEL_BLOCK, cn + NSA_CMP_BLOCK > jn), 1.0, 0.0).astype(BF16)
    tw = NSA_HPG * t
    diff = lax.broadcasted_iota(jnp.int32, (t, tw), 0) - (lax.broadcasted_iota(jnp.int32, (t, tw), 1) & (t - 1))
    causal = (lax.broadcasted_iota(jnp.int32, (t, t), 0) <= lax.broadcasted_iota(jnp.int32, (t, t), 1))
    cmp_end = lax.broadcasted_iota(jnp.int32, (ncr, tw), 0) * NSA_CMP_STRIDE + (NSA_CMP_BLOCK - 1)
    cmp_q = lax.broadcasted_iota(jnp.int32, (ncr, tw), 1) & (t - 1)
    blk_j = lax.broadcasted_iota(jnp.int32, (nsl, t), 0)
    ones = jnp.ones((ONES_ROWS, t), BF16)

    def q_tile(qi, _):
        q0 = pl.multiple_of(qi * t, t)
        qs = [qt_ref[hp * hd:(hp + 1) * hd, pl.ds(q0, t)] for hp in range(NSA_HPG)]
        q = jnp.concatenate(qs, axis=1)

        k1 = pl.multiple_of(jnp.maximum(qi - 1, 0) * t, t)
        k2 = pl.multiple_of(jnp.maximum(qi - 2, 0) * t, t)
        s_cmp = _dot(kc_ref[...], q)
        s_win = [_dot(kw_ref[pl.ds(k0, t), :], q) for k0 in (q0, k1, k2)]

        s = jnp.where(cmp_end <= q0 + cmp_q, s_cmp, NEG)
        e = jnp.exp2(s - jnp.max(s, axis=0, keepdims=True))
        den = jnp.sum(e, axis=0, keepdims=True)
        p = e * jnp.where(q0 + cmp_q[0:1, :] >= NSA_CMP_BLOCK - 1, 1.0 / den, 0.0)
        o_cmp = _dot(vct_ref[...], p.astype(BF16))
        p_sum = p[:, 0:t]
        for hp in range(1, NSA_HPG):
            p_sum = p_sum + p[:, hp * t:(hp + 1) * t]
        ps_hi = p_sum.astype(BF16)
        ps_lo = (p_sum - ps_hi.astype(F32)).astype(BF16)
        imp = _dot(ov_t, ps_hi) + _dot(ov_t, ps_lo)
        qpos = q0 + lax.broadcasted_iota(jnp.int32, (nsl, t), 1)
        back = qpos // NSA_SEL_BLOCK - blk_j
        forced = jnp.logical_or(blk_j == 0, jnp.logical_and(back >= 0, back < NSA_LOCAL_BLOCKS))
        valid = blk_j * NSA_SEL_BLOCK <= qpos
        score = jnp.where(valid, jnp.where(forced, FORCE, imp), NEG)
        n_grp = nsl // SUBLANES
        tiles = [score[a * SUBLANES:(a + 1) * SUBLANES] for a in range(n_grp)]
        sub = lax.broadcasted_iota(jnp.int32, (SUBLANES, t), 0)

        def count_group(mt, ranks):
            ranks = list(ranks)
            for r in range(SUBLANES):
                row = tiles[mt][r:r + 1, :]
                for a in range(n_grp):
                    if a < mt:
                        ahead = jnp.where(row > tiles[a], 1, 0)
                    elif a > mt:
                        ahead = jnp.where(row >= tiles[a], 1, 0)
                    else:
                        ahead = jnp.where(sub > r, jnp.where(row >= tiles[a], 1, 0), jnp.where(row > tiles[a], 1, 0))
                    ranks[a] = ranks[a] + ahead
            return tuple(ranks)

        ranks = tuple(jnp.zeros((SUBLANES, t), jnp.int32) for _ in range(n_grp))
        last_grp = (q0 + t - 1) // (NSA_SEL_BLOCK * SUBLANES)
        for mt in range(n_grp):
            ranks = lax.cond(mt <= last_grp, functools.partial(count_group, mt), lambda r: r, ranks)
        sel = jnp.logical_and(jnp.concatenate(ranks, axis=0) < n_top, valid)
        bias = jnp.where(sel, 0.0, NEG).astype(BF16)

        lo1 = jnp.where(qi >= 1, -t, t)
        lo2 = jnp.where(qi >= 2, 0, t)
        s_win = [jnp.where(diff <= 0, s_win[0], NEG), jnp.where(diff > lo1, s_win[1], NEG),
                 jnp.where(diff > lo2, s_win[2], NEG)]
        m_w = functools.reduce(jnp.maximum, [jnp.max(s, axis=0, keepdims=True) for s in s_win])
        acc_w = functools.reduce(jnp.add, [
            _dot(jnp.concatenate([vwt_ref[:, pl.ds(k0, t)], ones], axis=0), jnp.exp2(s - m_w).astype(BF16))
            for s, k0 in zip(s_win, (q0, k1, k2))])
        o_win = acc_w[:hd] / acc_w[hd:hd + 1]

        qa = [jnp.concatenate([qh, bias], axis=0) for qh in qs]
        for hp in range(NSA_HPG):
            ss_ref[0, hp] = jnp.where(causal, _dot(ks_ref[pl.ds(q0, t), :], qa[hp]), NEG)

        def score(hp, j):
            return _dot(ks_ref[pl.ds(pl.multiple_of(j * t, t), t), :], qa[hp])

        def value(hp, j):
            return jnp.concatenate([vst_ref[:, pl.ds(pl.multiple_of(j * t, t), t)], ones], axis=0)

        carries = _flash_pipelined(ss_ref, qi, qi, score, value, hd, t)
        for hp in range(NSA_HPG):
            _, acc = carries[hp]
            cols = slice(hp * t, (hp + 1) * t)
            gate = [gt_ref[pl.ds(gate_row0 + NSA_N_BRANCH * hp + br, 1), pl.ds(q0, t)] for br in range(NSA_N_BRANCH)]
            o = gate[0] * o_cmp[:, cols] + gate[1] * (acc[:hd] / acc[hd:hd + 1]) + gate[2] * o_win[:, cols]
            ot_ref[hp * hd:(hp + 1) * hd, pl.ds(q0, t)] = o.astype(ot_ref.dtype)
        return 0

    lax.fori_loop(0, seq // t, q_tile, 0)


def _nsa_attention(main_t, k_cmp, v_cmp_t, gates_t, *, batch, seq, row_ks, row_vs, row_kw, row_vw):
    groups, hd, hpg = NSA_KV_GROUPS, NSA_HEAD_DIM, NSA_HPG
    n = batch * seq
    ncr = seq // NSA_CMP_STRIDE
    nsl = seq // NSA_SEL_BLOCK
    gh = hpg * hd
    kv_rows = lambda row0: pl.BlockSpec((hd, seq), lambda b, g: (row0 // hd + g, b))
    return pl.pallas_call(
        functools.partial(_nsa_kernel, seq=seq),
        grid=(batch, groups),
        in_specs=[
            pl.BlockSpec((gh, seq), lambda b, g: (g, b)),
            pl.BlockSpec((None, ncr, hd), lambda b, g: (b * groups + g, 0, 0)),
            pl.BlockSpec((None, hd, ncr), lambda b, g: (b * groups + g, 0, 0)),
            kv_rows(row_ks), kv_rows(row_vs), kv_rows(row_kw), kv_rows(row_vw),
            pl.BlockSpec((gates_t.shape[0], seq), lambda b, g: (0, b)),
        ],
        out_specs=pl.BlockSpec((gh, seq), lambda b, g: (g, b)),
        out_shape=jax.ShapeDtypeStruct((groups * gh, n), BF16),
        scratch_shapes=[pltpu.VMEM((seq, hd + nsl), BF16), pltpu.VMEM((seq, hd), BF16),
                        pltpu.VMEM((2, hpg, ATTN_TILE, ATTN_TILE), F32)],
        compiler_params=_params("parallel", "parallel"),
        name="nsa_attention",
    )(main_t, k_cmp, v_cmp_t, main_t, main_t, main_t, main_t, gates_t)


def _nsa_layer(h, g, w_in, j, gate_b, q_gain, kc_gain, ks_gain, kw_gain, pos_k, pos_v, ck_w1, ck_w2, cv_w1, cv_w2, w_out,
               *, batch, seq):
    heads, groups, hd, hpg = NSA_HEADS, NSA_KV_GROUPS, NSA_HEAD_DIM, NSA_HPG
    assert seq % ATTN_TILE == 0 and NSA_WINDOW == 2 * ATTN_TILE and NSA_CMP_BLOCK == 2 * NSA_CMP_STRIDE
    qw, kvw = heads * hd, groups * hd
    r_kc, r_vc, r_ks, r_vs, r_kw, r_vw, r_g = (qw + i * kvw for i in range(7))
    n_gate = NSA_N_BRANCH * heads
    ones = jnp.ones((kvw,), F32)
    col = jnp.concatenate([jnp.tile(q_gain, heads) * (hd ** -0.5 * LOG2E), ones, ones, jnp.tile(ks_gain, groups), ones,
                           jnp.tile(kw_gain, groups), ones, gate_b])
    segs = ((0, qw, "norm"), (r_kc, r_ks, "raw"), (r_ks, r_vs, "norm"), (r_vs, r_kw, "raw"), (r_kw, r_vw, "norm"),
            (r_vw, r_g, "raw"), (r_g, r_g + n_gate, "sigmoid"))
    main_t, gates_t = _proj_t(h, g, w_in, j, col.reshape(-1, 1), segs, head_dim=hd)

    def token_major(r0):
        return main_t[r0:r0 + kvw].reshape(groups, hd, batch, seq).transpose(2, 0, 3, 1)

    stride = NSA_CMP_STRIDE
    rk = token_major(r_kc).reshape(batch * groups, seq // stride, stride * hd)
    rv = token_major(r_vc).reshape(batch * groups, seq // stride, stride * hd)

    def pos_rows(p):
        return jnp.broadcast_to(p.reshape(1, -1), (8, p.size)).astype(BF16)

    k_cmp, v_cmp_t = _nsa_compress(rk, rv, pos_rows(pos_k), pos_rows(pos_v), ck_w1.astype(BF16), ck_w2.astype(BF16),
                                   cv_w1.astype(BF16), cv_w2.T.astype(BF16), kc_gain)
    o_t = _nsa_attention(main_t, k_cmp, v_cmp_t, gates_t, batch=batch, seq=seq,
                         row_ks=r_ks, row_vs=r_vs, row_kw=r_kw, row_vw=r_vw)
    return o_t, w_out, j, "feature_major"


def kernel(x, norm_g, ffn1_wi, ffn1_wo, ffn2_wi, ffn2_wo, moba_w_in, moba_q_gain, moba_k_gain, moba_w_out, lru_w_in, lru_conv_w, lru_conv_b, lru_wa, lru_ba, lru_wx, lru_bx, lru_lam, lru_w_out, nsa_w_in, nsa_gate_b, nsa_q_gain, nsa_kc_gain, nsa_ks_gain, nsa_kw_gain, nsa_pos_k, nsa_pos_v, nsa_ck_w1, nsa_ck_w2, nsa_cv_w1, nsa_cv_w2, nsa_w_out):
    batch, seq, d = x.shape
    n_mixers = 3
    h = x.reshape(batch * seq, d)
    for i in range(norm_g.shape[0]):
        j, kind = divmod(i, n_mixers)
        h = _ffn(h, norm_g[i, 0], ffn1_wi, ffn1_wo, i)
        if kind == 0:
            mixed = _moba_layer(h, norm_g[i, 1], moba_w_in, j, moba_q_gain[j], moba_k_gain[j], moba_w_out,
                                batch=batch, seq=seq)
        elif kind == 1:
            mixed = _lru_layer(h, norm_g[i, 1], lru_w_in, j, lru_conv_w[j], lru_conv_b[j], lru_wa[j], lru_ba[j],
                               lru_wx[j], lru_bx[j], lru_lam[j], lru_w_out, batch=batch, seq=seq)
        else:
            mixed = _nsa_layer(h, norm_g[i, 1], nsa_w_in, j, nsa_gate_b[j], nsa_q_gain[j], nsa_kc_gain[j],
                               nsa_ks_gain[j], nsa_kw_gain[j], nsa_pos_k[j], nsa_pos_v[j], nsa_ck_w1[j], nsa_ck_w2[j],
                               nsa_cv_w1[j], nsa_cv_w2[j], nsa_w_out, batch=batch, seq=seq)
        h = _ffn(h, norm_g[i, 2], ffn2_wi, ffn2_wo, i, mixed=mixed)
    return h.reshape(batch, seq, d)
```

```python
import functools

import jax
import jax.numpy as jnp
from jax import lax
from jax.experimental import pallas as pl
from jax.experimental.pallas import tpu as pltpu

F32 = jnp.float32
BF16 = jnp.bfloat16

EPS = 1e-6
NEG = -1e30
FORCE = 1e30
FFN_RESID = 0.5

MOBA_HEADS = 16
MOBA_BLOCK = 256
MOBA_TOPK = 3

LRU_BLOCKS = 4
CONV_WIDTH = 4
LRU_C = 8.0

NSA_HEADS = 16
NSA_KV_GROUPS = 4
NSA_HEAD_DIM = 64
NSA_HPG = NSA_HEADS // NSA_KV_GROUPS
NSA_CMP_BLOCK = 32
NSA_CMP_STRIDE = 16
NSA_SEL_BLOCK = 64
NSA_TOPN = 16
NSA_LOCAL_BLOCKS = 2
NSA_WINDOW = 512
NSA_N_BRANCH = 3

V7X_VMEM_BYTES = 64 * 1024 * 1024
VMEM_LIMIT_BYTES = V7X_VMEM_BYTES - 8 * 1024 * 1024

LANES = 128
SUBLANES = 8
LOG2E = 1.4426950408889634

ATTN_TILE = 256


def _params(*sem):
    return pltpu.CompilerParams(dimension_semantics=sem, vmem_limit_bytes=VMEM_LIMIT_BYTES)


def _resident(shape):
    nd = len(shape)
    return pl.BlockSpec(shape, lambda *_: (0,) * nd, pipeline_mode=pl.Buffered(1))


def _layer_block(stack, layer):
    nd = stack.ndim
    return pl.BlockSpec((None,) + stack.shape[1:], lambda *_: (layer,) + (0,) * (nd - 1), pipeline_mode=pl.Buffered(1))


def _rms_rows(x, g):
    return x * lax.rsqrt(jnp.mean(x * x, axis=-1, keepdims=True) + EPS) * g


def _gelu_tanh(x):
    c = 0.7978845608028654
    return 0.5 * x * (1.0 + jnp.tanh(c * (x + 0.044715 * (x * x * x))))


def _dot(a, b):
    return jnp.dot(a, b, preferred_element_type=F32)


def _dot_nt(a, b):
    return lax.dot_general(a, b, (((1,), (1,)), ((), ())), preferred_element_type=F32)


def _ffn_kernel(x_ref, *refs, d_ff, chunk, mix):
    if mix is None:
        g_ref, wi_ref, wo_ref, o_ref = refs
        x = x_ref[...]
    else:
        a_ref, wm_ref, g_ref, wi_ref, wo_ref, o_ref = refs
        contract = (((0,), (0,)), ((), ())) if mix == "feature_major" else (((1,), (0,)), ((), ()))
        x = x_ref[...] + lax.dot_general(a_ref[...], wm_ref[...].astype(BF16), contract, preferred_element_type=F32)
    xn = _rms_rows(x, g_ref[...]).astype(BF16)
    acc = jnp.zeros(x.shape, F32)
    for c0 in range(0, d_ff, chunk):
        gate = _dot(xn, wi_ref[:, c0:c0 + chunk].astype(BF16))
        up = _dot(xn, wi_ref[:, d_ff + c0:d_ff + c0 + chunk].astype(BF16))
        act = (gate * jax.nn.sigmoid(gate) * up).astype(BF16)
        acc = acc + _dot(act, wo_ref[c0:c0 + chunk, :].astype(BF16))
    o_ref[...] = x + FFN_RESID * acc


def _ffn(h, g, wi, wo, layer, *, mixed=None, tm=512, chunk=256):
    n, d = h.shape
    d_ff = wo.shape[1]
    row = pl.BlockSpec((tm, d), lambda i: (i, 0))
    operands, specs, mix = [h], [row], None
    if mixed is not None:
        a, w_mix, index, mix = mixed
        k = w_mix.shape[1]
        a_spec = pl.BlockSpec((k, tm), lambda i: (0, i)) if mix == "feature_major" else pl.BlockSpec((tm, k), lambda i: (i, 0))
        operands += [a, w_mix]
        specs += [a_spec, _layer_block(w_mix, index)]
    return pl.pallas_call(
        functools.partial(_ffn_kernel, d_ff=d_ff, chunk=chunk, mix=mix),
        grid=(n // tm,),
        in_specs=specs + [_resident((1, d)), _layer_block(wi, layer), _layer_block(wo, layer)],
        out_specs=row,
        out_shape=jax.ShapeDtypeStruct((n, d), F32),
        compiler_params=_params("parallel"),
        name="ffn",
    )(*operands, g.reshape(1, d), wi, wo)


PROJ_XPOSE_COLS = 512


def _proj_t_kernel(x_ref, g_ref, w_ref, col_ref, *refs, segs, head_dim, n_main):
    if len(refs) == 2:
        (o_ref, wt_ref), tail_ref, aux_ref = refs, None, None
    else:
        tail_ref, o_ref, aux_ref, wt_ref = refs

    @pl.when(pl.program_id(0) == 0)
    def _():
        for c0 in range(0, n_main, PROJ_XPOSE_COLS):
            wt_ref[c0:c0 + PROJ_XPOSE_COLS, :] = w_ref[:, c0:c0 + PROJ_XPOSE_COLS].T.astype(BF16)

    xn = _rms_rows(x_ref[...], g_ref[...]).astype(BF16)
    tm = xn.shape[0]
    for r0, r1, mode in segs:
        acc = _dot_nt(tail_ref[...] if mode == "sigmoid" else wt_ref[r0:r1, :], xn)
        if mode == "raw":
            o_ref[r0:r1, :] = acc.astype(o_ref.dtype)
        elif mode == "norm":
            nh = (r1 - r0) // head_dim
            a3 = acc.reshape(nh, head_dim, tm)
            ms = jnp.mean(a3 * a3, axis=1, keepdims=True)
            y = a3 * lax.rsqrt(ms + EPS) * col_ref[r0:r1, :].reshape(nh, head_dim, 1)
            o_ref[r0:r1, :] = y.reshape(r1 - r0, tm).astype(o_ref.dtype)
        else:
            aux_ref[...] = jax.nn.sigmoid(acc + col_ref[r0:r1, :])


def _proj_t(h, g, w, layer, col, segs, *, head_dim, tm=512):
    n, d = h.shape
    rows = w.shape[2]
    n_main = max(r1 for _, r1, mode in segs if mode != "sigmoid")
    aux = [(r0, r1) for r0, r1, mode in segs if mode == "sigmoid"]
    assert n_main % PROJ_XPOSE_COLS == 0
    operands = [h, g.reshape(1, d), w, col]
    in_specs = [pl.BlockSpec((tm, d), lambda i: (i, 0)), _resident((1, d)), _layer_block(w, layer), _resident((rows, 1))]
    out_shape = [jax.ShapeDtypeStruct((n_main, n), BF16)]
    out_specs = [pl.BlockSpec((n_main, tm), lambda i: (0, i))]
    if aux:
        ((a0, a1),) = aux
        operands.append(w[layer, :, a0:a1].T.astype(BF16))
        in_specs.append(_resident((a1 - a0, d)))
        out_shape.append(jax.ShapeDtypeStruct((a1 - a0, n), F32))
        out_specs.append(pl.BlockSpec((a1 - a0, tm), lambda i: (0, i)))
    return pl.pallas_call(
        functools.partial(_proj_t_kernel, segs=segs, head_dim=head_dim, n_main=n_main),
        grid=(n // tm,),
        in_specs=in_specs,
        out_specs=out_specs,
        out_shape=out_shape,
        scratch_shapes=[pltpu.VMEM((n_main, d), BF16)],
        compiler_params=_params("arbitrary"),
        name="proj_t",
    )(*operands)


ONES_ROWS = 16
FLASH_LEAD = 3


def _flash_pipelined(ss_ref, n_past, diag_tile, score_fn, value_fn, hd, width):
    n_chain = ss_ref.shape[1]
    n_step = n_past + 1
    last = jnp.maximum(n_past - 1, 0)
    init = tuple((jnp.full((1, width), NEG, F32), jnp.zeros((hd + ONES_ROWS, width), F32)) for _ in range(n_chain))

    def step(i, carries, slot, prefetch):
        nxt = jnp.minimum(i, last)
        cur = jnp.where(i == 0, diag_tile, i - 1)
        if prefetch:
            for c in range(min(FLASH_LEAD, n_chain)):
                ss_ref[1 - slot, c] = score_fn(c, nxt)
        out = []
        for c in range(n_chain):
            s = ss_ref[slot, c]
            m, acc = carries[c]
            m_new = jnp.maximum(m, jnp.max(s, axis=0, keepdims=True))
            p = jnp.exp2(s - m_new).astype(BF16)
            out.append((m_new, jnp.exp2(m - m_new) * acc + _dot(value_fn(c, cur), p)))
            if prefetch and c + FLASH_LEAD < n_chain:
                ss_ref[1 - slot, c + FLASH_LEAD] = score_fn(c + FLASH_LEAD, nxt)
        return tuple(out)

    def two_steps(k, carries):
        return step(2 * k + 1, step(2 * k, carries, 0, True), 1, True)

    carries = lax.fori_loop(0, n_step // 2, two_steps, init)
    return lax.cond(n_step % 2 == 1, lambda c: step(n_step - 1, c, 0, False), lambda c: c, carries)


def _moba_kernel(qt_ref, kt_ref, vt_ref, ot_ref, ka_ref, ss_ref, *, seq, hb, hd):
    t = ATTN_TILE
    nb = seq // t
    kw = ka_ref.shape[-1]
    rows = [slice(h * hd, (h + 1) * hd) for h in range(hb)]
    col = lax.broadcasted_iota(jnp.int32, (t, kw - hd), 1)

    def fill_keys(j, _):
        j0 = pl.multiple_of(j * t, t)
        onehot = jnp.where(col == j, 1.0, 0.0).astype(BF16)
        for h in range(hb):
            ka_ref[h, pl.ds(j0, t), :] = jnp.concatenate([kt_ref[rows[h], pl.ds(j0, t)].T, onehot], axis=1)
        return 0

    lax.fori_loop(0, nb, fill_keys, 0)
    avg = jnp.where(lax.broadcasted_iota(jnp.int32, (nb, seq), 1) // t
                    == lax.broadcasted_iota(jnp.int32, (nb, seq), 0), 1.0 / t, 0.0).astype(BF16)
    km = []
    for h in range(hb):
        kmean = _dot(avg, ka_ref[h])
        km_hi = kmean.astype(BF16)
        km.append((km_hi, (kmean - km_hi.astype(F32)).astype(BF16)))
    blk = lax.broadcasted_iota(jnp.int32, (nb, t), 0)
    causal = (lax.broadcasted_iota(jnp.int32, (t, t), 0) <= lax.broadcasted_iota(jnp.int32, (t, t), 1))
    ones = jnp.ones((ONES_ROWS, t), BF16)

    def q_tile(qi, _):
        q0 = pl.multiple_of(qi * t, t)
        qs = [qt_ref[r, pl.ds(q0, t)] for r in rows]
        qz = [jnp.concatenate([q, jnp.zeros((kw - hd, t), BF16)], axis=0) for q in qs]
        gates = [_dot(km[h][0], qz[h]) + _dot(km[h][1], qz[h]) for h in range(hb)]
        for h in range(hb):
            ss_ref[0, h] = jnp.where(causal, _dot(ka_ref[h, pl.ds(q0, t), :], qz[h]), NEG)
        past = blk < qi
        qa = []
        for h in range(hb):
            gate = jnp.where(past, gates[h], NEG)
            sel = jnp.zeros((nb, t), jnp.bool_)
            for _ in range(min(MOBA_TOPK, nb)):
                top = jnp.max(gate, axis=0, keepdims=True)
                first = jnp.min(jnp.where(gate == top, blk, nb), axis=0, keepdims=True)
                hit = blk == first
                sel = jnp.logical_or(sel, hit)
                gate = jnp.where(hit, -jnp.inf, gate)
            bias = jnp.where(jnp.logical_and(sel, past), 0.0, NEG).astype(BF16)
            qa.append(jnp.concatenate([qs[h], bias, jnp.zeros((kw - hd - nb, t), BF16)], axis=0))

        def score(h, j):
            return _dot(ka_ref[h, pl.ds(pl.multiple_of(j * t, t), t), :], qa[h])

        def value(h, j):
            return jnp.concatenate([vt_ref[rows[h], pl.ds(pl.multiple_of(j * t, t), t)], ones], axis=0)

        carries = _flash_pipelined(ss_ref, qi, qi, score, value, hd, t)
        for h in range(hb):
            _, acc = carries[h]
            ot_ref[rows[h], pl.ds(q0, t)] = (acc[:hd] / acc[hd:hd + 1]).astype(ot_ref.dtype)
        return 0

    lax.fori_loop(0, nb, q_tile, 0)


MOBA_HEADS_PER_STEP = 8


def _moba_attention(qkv_t, *, batch, seq, heads, head_dim):
    n = batch * seq
    hb = MOBA_HEADS_PER_STEP
    hg = heads // hb
    blk = lambda third: pl.BlockSpec((hb * head_dim, seq), lambda b, h: (third * hg + h, b))
    return pl.pallas_call(
        functools.partial(_moba_kernel, seq=seq, hb=hb, hd=head_dim),
        grid=(batch, hg),
        in_specs=[blk(0), blk(1), blk(2)],
        out_specs=blk(0),
        out_shape=jax.ShapeDtypeStruct((heads * head_dim, n), BF16),
        scratch_shapes=[pltpu.VMEM((hb, seq, LANES), BF16), pltpu.VMEM((2, hb, ATTN_TILE, ATTN_TILE), F32)],
        compiler_params=_params("parallel", "parallel"),
        name="moba_attention",
    )(qkv_t, qkv_t, qkv_t)


def _moba_layer(h, g, w_in, j, q_gain, k_gain, w_out, *, batch, seq):
    heads = MOBA_HEADS
    hd = w_in.shape[2] // (3 * heads)
    hw = heads * hd
    assert seq % MOBA_BLOCK == 0 and MOBA_BLOCK == ATTN_TILE and hd + seq // MOBA_BLOCK <= LANES
    col = jnp.concatenate([jnp.tile(q_gain, heads) * (hd ** -0.5 * LOG2E), jnp.tile(k_gain, heads), jnp.ones((hw,), F32)])
    segs = ((0, hw, "norm"), (hw, 2 * hw, "norm"), (2 * hw, 3 * hw, "raw"))
    (qkv_t,) = _proj_t(h, g, w_in, j, col.reshape(-1, 1), segs, head_dim=hd)
    o_t = _moba_attention(qkv_t, batch=batch, seq=seq, heads=heads, head_dim=hd)
    return o_t, w_out, j, "feature_major"


def _lru_proj_kernel(x_ref, g_ref, w_ref, xb_ref, y_ref, wb_ref, *, width):
    @pl.when(pl.program_id(0) == 0)
    def _():
        wb_ref[...] = w_ref[...].astype(BF16)

    xn = _rms_rows(x_ref[...], g_ref[...]).astype(BF16)
    xb_ref[...] = _dot(xn, wb_ref[:, :width])
    y_ref[...] = _gelu_tanh(_dot(xn, wb_ref[:, width:]))


def _lru_proj(h, g, w, layer, *, tm=512):
    n, d = h.shape
    width = w.shape[2] // 2
    return pl.pallas_call(
        functools.partial(_lru_proj_kernel, width=width),
        grid=(n // tm,),
        in_specs=[pl.BlockSpec((tm, d), lambda i: (i, 0)), _resident((1, d)), _layer_block(w, layer)],
        out_specs=[pl.BlockSpec((tm, width), lambda i: (i, 0))] * 2,
        out_shape=[jax.ShapeDtypeStruct((n, width), F32)] * 2,
        scratch_shapes=[pltpu.VMEM(w.shape[1:], BF16)],
        compiler_params=_params("arbitrary"),
        name="lru_proj",
    )(h, g.reshape(1, d), w)


LRU_CONV_PAD = 8


def _lru_kernel(xb_ref, y_ref, cw_ref, cb_ref, wa_ref, ba_ref, wx_ref, bx_ref, lam_ref, o_ref,
                xpad, a_s, b_s, h_s, h_carry, *, tile, width):
    si = pl.program_id(1)
    pad = LRU_CONV_PAD

    @pl.when(si == 0)
    def _():
        xpad[0:pad, :] = jnp.zeros((pad, width), F32)
        h_carry[...] = jnp.zeros((1, width), F32)

    xpad[pad:pad + tile, :] = xb_ref[...]
    xc = cb_ref[...] + cw_ref[0:1, :] * xpad[pad - CONV_WIDTH + 1:pad - CONV_WIDTH + 1 + tile, :]
    for k in range(1, CONV_WIDTH):
        off = pad - CONV_WIDTH + 1 + k
        xc = xc + cw_ref[k:k + 1, :] * xpad[off:off + tile, :]
    xcb = xc.astype(BF16)
    first = (lax.broadcasted_iota(jnp.int32, (tile, 1), 0) + si * tile) == 0
    bw = width // LRU_BLOCKS
    for n in range(LRU_BLOCKS):
        sl = slice(n * bw, (n + 1) * bw)
        r = jax.nn.sigmoid(_dot(xcb[:, sl], wa_ref[n]) + ba_ref[:, sl])
        i = jax.nn.sigmoid(_dot(xcb[:, sl], wx_ref[n]) + bx_ref[:, sl])
        z = -lam_ref[:, sl]
        softplus = jnp.maximum(z, 0.0) + jnp.log1p(jnp.exp(-jnp.abs(z)))
        a = jnp.exp(-LRU_C * r * softplus)
        mult = jnp.where(first, 1.0, jnp.sqrt(1.0 - a * a))
        a_s[:, sl] = a
        b_s[:, sl] = mult * i * xc[:, sl]

    def step(t, h):
        h = a_s[pl.ds(t, 1), :] * h + b_s[pl.ds(t, 1), :]
        h_s[pl.ds(t, 1), :] = h
        return h

    h_carry[...] = lax.fori_loop(0, tile, step, h_carry[...], unroll=8)
    o_ref[...] = (h_s[...] * y_ref[...]).astype(o_ref.dtype)
    xpad[0:pad, :] = xpad[tile:tile + pad, :]


def _lru_recurrence(xb, y, cw, cb, wa, ba, wx, bx, lam, *, batch, seq, tile=512):
    n, width = xb.shape
    nt = seq // tile
    row = lambda b, s: (b * nt + s, 0)
    vec = _resident((1, width))
    return pl.pallas_call(
        functools.partial(_lru_kernel, tile=tile, width=width),
        grid=(batch, nt),
        in_specs=[
            pl.BlockSpec((tile, width), row),
            pl.BlockSpec((tile, width), row),
            _resident(cw.shape), vec,
            _resident(wa.shape), vec,
            _resident(wx.shape), vec,
            vec,
        ],
        out_specs=pl.BlockSpec((tile, width), row),
        out_shape=jax.ShapeDtypeStruct((n, width), BF16),
        scratch_shapes=[
            pltpu.VMEM((tile + LRU_CONV_PAD, width), F32),
            pltpu.VMEM((tile, width), F32),
            pltpu.VMEM((tile, width), F32),
            pltpu.VMEM((tile, width), F32),
            pltpu.VMEM((1, width), F32),
        ],
        compiler_params=_params("parallel", "arbitrary"),
        name="lru_recurrence",
    )(xb, y, cw, cb.reshape(1, width), wa, ba.reshape(1, width), wx, bx.reshape(1, width), lam.reshape(1, width))


def _lru_layer(h, g, w_in, j, conv_w, conv_b, wa, ba, wx, bx, lam, w_out, *, batch, seq):
    xb, y = _lru_proj(h, g, w_in, j)
    gated = _lru_recurrence(xb, y, conv_w.reshape(CONV_WIDTH, -1), conv_b, wa.astype(BF16), ba, wx.astype(BF16), bx, lam,
                            batch=batch, seq=seq)
    return gated, w_out, j, "token_major"


def _nsa_compress_kernel(kct_ref, vct_ref, pos_ref, w1_ref, kw2_ref, vw2t_ref, kg_ref, ko_ref, vot_ref, tok_ref, *, seq):
    t = ATTN_TILE
    rows = seq // NSA_CMP_STRIDE
    hid = kw2_ref.shape[0]

    def fill_tokens(j, _):
        j0 = pl.multiple_of(j * t, t)
        kv = jnp.concatenate([kct_ref[:, pl.ds(j0, t)], vct_ref[:, pl.ds(j0, t)]], axis=0)
        tok_ref[pl.ds(j0, t), :] = kv.T.astype(F32)
        return 0

    lax.fori_loop(0, seq // t, fill_tokens, 0)
    tok_ref[seq:seq + NSA_CMP_STRIDE, :] = jnp.zeros((NSA_CMP_STRIDE, tok_ref.shape[1]), F32)

    def position(p):
        return (tok_ref[pl.ds(p, rows, stride=NSA_CMP_STRIDE), :] + pos_ref[p:p + 1, :]).astype(BF16)

    acc = jnp.zeros((rows, 2 * hid), F32)
    for p in range(0, NSA_CMP_BLOCK, 2):
        acc = acc + _dot(jnp.concatenate([position(p), position(p + 1)], axis=1), w1_ref[p // 2])
    hidden = _gelu_tanh(acc).astype(BF16)
    ko_ref[...] = _rms_rows(_dot(hidden[:, :hid], kw2_ref[...]), kg_ref[...]).astype(ko_ref.dtype)
    vot_ref[...] = _dot_nt(vw2t_ref[...], hidden[:, hid:]).astype(vot_ref.dtype)


def _nsa_compress(main_t, pos, w1_pairs, kw2, vw2t, k_gain, *, batch, seq, row_kc, row_vc):
    groups, hd = NSA_KV_GROUPS, NSA_HEAD_DIM
    bg, rows = batch * groups, seq // NSA_CMP_STRIDE
    kv_rows = lambda row0: pl.BlockSpec((hd, seq), lambda i: (row0 // hd + i % groups, i // groups))
    return pl.pallas_call(
        functools.partial(_nsa_compress_kernel, seq=seq),
        grid=(bg,),
        in_specs=[kv_rows(row_kc), kv_rows(row_vc), _resident(pos.shape), _resident(w1_pairs.shape), _resident(kw2.shape),
                  _resident(vw2t.shape), _resident((1, hd))],
        out_specs=[pl.BlockSpec((None, rows, hd), lambda i: (i, 0, 0)), pl.BlockSpec((None, hd, rows), lambda i: (i, 0, 0))],
        out_shape=[jax.ShapeDtypeStruct((bg, rows, hd), BF16), jax.ShapeDtypeStruct((bg, hd, rows), BF16)],
        scratch_shapes=[pltpu.VMEM((seq + NSA_CMP_STRIDE, 2 * hd), F32)],
        compiler_params=_params("parallel"),
        name="nsa_compress",
    )(main_t, main_t, pos, w1_pairs, kw2, vw2t, k_gain.reshape(1, hd))


def _nsa_kernel(qt_ref, kc_ref, vct_ref, kst_ref, vst_ref, kwt_ref, vwt_ref, gt_ref, ot_ref, ks_ref, kw_ref, ss_ref,
                *, seq):
    t = ATTN_TILE
    hd = NSA_HEAD_DIM
    ncr = seq // NSA_CMP_STRIDE
    nsl = seq // NSA_SEL_BLOCK
    n_top = min(NSA_TOPN, nsl)
    ones_col = jnp.where(lax.broadcasted_iota(jnp.int32, (t, hd), 1) == 0, 1.0, 0.0).astype(BF16)
    sel_col = lax.broadcasted_iota(jnp.int32, (t, nsl), 1)
    sel_blk = lax.broadcasted_iota(jnp.int32, (t, nsl), 0) // NSA_SEL_BLOCK

    def fill_keys(j, _):
        j0 = pl.multiple_of(j * t, t)
        onehot = jnp.where(sel_col == j * (t // NSA_SEL_BLOCK) + sel_blk, 1.0, 0.0).astype(BF16)
        ks_ref[pl.ds(j0, t), :] = jnp.concatenate([kst_ref[:, pl.ds(j0, t)].T, onehot], axis=1)
        kw_ref[pl.ds(j0, t), :] = jnp.concatenate([kwt_ref[:, pl.ds(j0, t)].T, ones_col], axis=1)
        return 0

    lax.fori_loop(0, seq // t, fill_keys, 0)
    gate_row0 = pl.program_id(1) * (NSA_HPG * NSA_N_BRANCH)
    jn = lax.broadcasted_iota(jnp.int32, (nsl, ncr), 0) * NSA_SEL_BLOCK
    cn = lax.broadcasted_iota(jnp.int32, (nsl, ncr), 1) * NSA_CMP_STRIDE
    ov_t = jnp.where(jnp.logical_and(cn < jn + NSA_SEL_BLOCK, cn + NSA_CMP_BLOCK > jn), 1.0, 0.0).astype(BF16)
    tw = NSA_HPG * t
    diff = lax.broadcasted_iota(jnp.int32, (t, tw), 0) - (lax.broadcasted_iota(jnp.int32, (t, tw), 1) & (t - 1))
    causal = (lax.broadcasted_iota(jnp.int32, (t, t), 0) <= lax.broadcasted_iota(jnp.int32, (t, t), 1))
    cmp_end = lax.broadcasted_iota(jnp.int32, (ncr, tw), 0) * NSA_CMP_STRIDE + (NSA_CMP_BLOCK - 1)
    cmp_q = lax.broadcasted_iota(jnp.int32, (ncr, tw), 1) & (t - 1)
    blk_j = lax.broadcasted_iota(jnp.int32, (nsl, t), 0)
    ones = jnp.ones((ONES_ROWS, t), BF16)

    def q_tile(qi, _):
        q0 = pl.multiple_of(qi * t, t)
        qs = [qt_ref[hp * hd:(hp + 1) * hd, pl.ds(q0, t)] for hp in range(NSA_HPG)]
        q = jnp.concatenate(qs, axis=1)

        k1 = pl.multiple_of(jnp.maximum(qi - 1, 0) * t, t)
        k2 = pl.multiple_of(jnp.maximum(qi - 2, 0) * t, t)
        s_cmp = _dot(kc_ref[...], q)
        row0 = lax.broadcasted_iota(jnp.int32, (hd, tw), 0) == 0
        q_pen = [jnp.concatenate([q, jnp.where(row0, pen, 0.0).astype(BF16)], axis=0)
                 for pen in (0.0, jnp.where(qi >= 1, 0.0, NEG), jnp.where(qi >= 2, 0.0, NEG))]
        s_win = [_dot(kw_ref[pl.ds(k0, t), :], qp) for k0, qp in zip((q0, k1, k2), q_pen)]

        s = jnp.where(cmp_end <= q0 + cmp_q, s_cmp, NEG)
        e = jnp.exp2(s - jnp.max(s, axis=0, keepdims=True))
        den = jnp.sum(e, axis=0, keepdims=True)
        p = e * jnp.where(q0 + cmp_q[0:1, :] >= NSA_CMP_BLOCK - 1, 1.0 / den, 0.0)
        o_cmp = _dot(vct_ref[...], p.astype(BF16))
        p_sum = p[:, 0:t]
        for hp in range(1, NSA_HPG):
            p_sum = p_sum + p[:, hp * t:(hp + 1) * t]
        ps_hi = p_sum.astype(BF16)
        ps_lo = (p_sum - ps_hi.astype(F32)).astype(BF16)
        imp = _dot(ov_t, ps_hi) + _dot(ov_t, ps_lo)
        qpos = q0 + lax.broadcasted_iota(jnp.int32, (nsl, t), 1)
        back = qpos // NSA_SEL_BLOCK - blk_j
        forced = jnp.logical_or(blk_j == 0, jnp.logical_and(back >= 0, back < NSA_LOCAL_BLOCKS))
        valid = blk_j * NSA_SEL_BLOCK <= qpos
        score = jnp.where(valid, jnp.where(forced, FORCE, imp), NEG)
        n_grp = nsl // SUBLANES
        tiles = [score[a * SUBLANES:(a + 1) * SUBLANES] for a in range(n_grp)]
        sub = lax.broadcasted_iota(jnp.int32, (SUBLANES, t), 0)

        def count_group(mt, ranks):
            ranks = list(ranks)
            for r in range(SUBLANES):
                row = tiles[mt][r:r + 1, :]
                for a in range(n_grp):
                    if a < mt:
                        ahead = jnp.where(row > tiles[a], 1, 0)
                    elif a > mt:
                        ahead = jnp.where(row >= tiles[a], 1, 0)
                    else:
                        ahead = jnp.where(sub > r, jnp.where(row >= tiles[a], 1, 0), jnp.where(row > tiles[a], 1, 0))
                    ranks[a] = ranks[a] + ahead
            return tuple(ranks)

        ranks = tuple(jnp.zeros((SUBLANES, t), jnp.int32) for _ in range(n_grp))
        last_grp = (q0 + t - 1) // (NSA_SEL_BLOCK * SUBLANES)
        for mt in range(n_grp):
            ranks = lax.cond(mt <= last_grp, functools.partial(count_group, mt), lambda r: r, ranks)
        sel = jnp.logical_and(jnp.concatenate(ranks, axis=0) < n_top, valid)
        bias = jnp.where(sel, 0.0, NEG).astype(BF16)

        lower = diff <= 0
        s_x = jnp.where(lower, s_win[0], s_win[2])
        m_w = jnp.maximum(jnp.max(s_x, axis=0, keepdims=True), jnp.max(s_win[1], axis=0, keepdims=True))
        p_x = jnp.exp2(s_x - m_w)
        p_win = [jnp.where(lower, p_x, 0.0), jnp.exp2(s_win[1] - m_w), jnp.where(lower, 0.0, p_x)]
        acc_w = functools.reduce(jnp.add, [
            _dot(jnp.concatenate([vwt_ref[:, pl.ds(k0, t)], ones], axis=0), p.astype(BF16))
            for p, k0 in zip(p_win, (q0, k1, k2))])
        o_win = acc_w[:hd] / acc_w[hd:hd + 1]

        qa = [jnp.concatenate([qh, bias], axis=0) for qh in qs]
        for hp in range(NSA_HPG):
            ss_ref[0, hp] = jnp.where(causal, _dot(ks_ref[pl.ds(q0, t), :], qa[hp]), NEG)

        def score(hp, j):
            return _dot(ks_ref[pl.ds(pl.multiple_of(j * t, t), t), :], qa[hp])

        def value(hp, j):
            return jnp.concatenate([vst_ref[:, pl.ds(pl.multiple_of(j * t, t), t)], ones], axis=0)

        carries = _flash_pipelined(ss_ref, qi, qi, score, value, hd, t)
        for hp in range(NSA_HPG):
            _, acc = carries[hp]
            cols = slice(hp * t, (hp + 1) * t)
            gate = [gt_ref[pl.ds(gate_row0 + NSA_N_BRANCH * hp + br, 1), pl.ds(q0, t)] for br in range(NSA_N_BRANCH)]
            o = gate[0] * o_cmp[:, cols] + gate[1] * (acc[:hd] / acc[hd:hd + 1]) + gate[2] * o_win[:, cols]
            ot_ref[hp * hd:(hp + 1) * hd, pl.ds(q0, t)] = o.astype(ot_ref.dtype)
        return 0

    lax.fori_loop(0, seq // t, q_tile, 0)


def _nsa_attention(main_t, k_cmp, v_cmp_t, gates_t, *, batch, seq, row_ks, row_vs, row_kw, row_vw):
    groups, hd, hpg = NSA_KV_GROUPS, NSA_HEAD_DIM, NSA_HPG
    n = batch * seq
    ncr = seq // NSA_CMP_STRIDE
    nsl = seq // NSA_SEL_BLOCK
    gh = hpg * hd
    kv_rows = lambda row0: pl.BlockSpec((hd, seq), lambda b, g: (row0 // hd + g, b))
    return pl.pallas_call(
        functools.partial(_nsa_kernel, seq=seq),
        grid=(batch, groups),
        in_specs=[
            pl.BlockSpec((gh, seq), lambda b, g: (g, b)),
            pl.BlockSpec((None, ncr, hd), lambda b, g: (b * groups + g, 0, 0)),
            pl.BlockSpec((None, hd, ncr), lambda b, g: (b * groups + g, 0, 0)),
            kv_rows(row_ks), kv_rows(row_vs), kv_rows(row_kw), kv_rows(row_vw),
            pl.BlockSpec((gates_t.shape[0], seq), lambda b, g: (0, b)),
        ],
        out_specs=pl.BlockSpec((gh, seq), lambda b, g: (g, b)),
        out_shape=jax.ShapeDtypeStruct((groups * gh, n), BF16),
        scratch_shapes=[pltpu.VMEM((seq, hd + nsl), BF16), pltpu.VMEM((seq, 2 * hd), BF16),
                        pltpu.VMEM((2, hpg, ATTN_TILE, ATTN_TILE), F32)],
        compiler_params=_params("parallel", "parallel"),
        name="nsa_attention",
    )(main_t, k_cmp, v_cmp_t, main_t, main_t, main_t, main_t, gates_t)


def _nsa_layer(h, g, w_in, j, gate_b, q_gain, kc_gain, ks_gain, kw_gain, pos_k, pos_v, ck_w1, ck_w2, cv_w1, cv_w2, w_out,
               *, batch, seq):
    heads, groups, hd, hpg = NSA_HEADS, NSA_KV_GROUPS, NSA_HEAD_DIM, NSA_HPG
    assert seq % ATTN_TILE == 0 and NSA_WINDOW == 2 * ATTN_TILE and NSA_CMP_BLOCK == 2 * NSA_CMP_STRIDE
    qw, kvw = heads * hd, groups * hd
    r_kc, r_vc, r_ks, r_vs, r_kw, r_vw, r_g = (qw + i * kvw for i in range(7))
    n_gate = NSA_N_BRANCH * heads
    ones = jnp.ones((kvw,), F32)
    col = jnp.concatenate([jnp.tile(q_gain, heads) * (hd ** -0.5 * LOG2E), ones, ones, jnp.tile(ks_gain, groups), ones,
                           jnp.tile(kw_gain, groups), ones, gate_b])
    segs = ((0, qw, "norm"), (r_kc, r_ks, "raw"), (r_ks, r_vs, "norm"), (r_vs, r_kw, "raw"), (r_kw, r_vw, "norm"),
            (r_vw, r_g, "raw"), (r_g, r_g + n_gate, "sigmoid"))
    main_t, gates_t = _proj_t(h, g, w_in, j, col.reshape(-1, 1), segs, head_dim=hd)

    l, hid = NSA_CMP_BLOCK, ck_w1.shape[1]
    zero = jnp.zeros((l, hd, hid), F32)
    w1 = jnp.concatenate([jnp.concatenate([ck_w1.reshape(l, hd, hid), zero], axis=2),
                          jnp.concatenate([zero, cv_w1.reshape(l, hd, hid)], axis=2)], axis=1)
    w1_pairs = w1.reshape(l // 2, 4 * hd, 2 * hid).astype(BF16)
    k_cmp, v_cmp_t = _nsa_compress(main_t, jnp.concatenate([pos_k, pos_v], axis=1), w1_pairs, ck_w2.astype(BF16),
                                   cv_w2.T.astype(BF16), kc_gain, batch=batch, seq=seq, row_kc=r_kc, row_vc=r_vc)
    o_t = _nsa_attention(main_t, k_cmp, v_cmp_t, gates_t, batch=batch, seq=seq,
                         row_ks=r_ks, row_vs=r_vs, row_kw=r_kw, row_vw=r_vw)
    return o_t, w_out, j, "feature_major"


def kernel(x, norm_g, ffn1_wi, ffn1_wo, ffn2_wi, ffn2_wo, moba_w_in, moba_q_gain, moba_k_gain, moba_w_out, lru_w_in, lru_conv_w, lru_conv_b, lru_wa, lru_ba, lru_wx, lru_bx, lru_lam, lru_w_out, nsa_w_in, nsa_gate_b, nsa_q_gain, nsa_kc_gain, nsa_ks_gain, nsa_kw_gain, nsa_pos_k, nsa_pos_v, nsa_ck_w1, nsa_ck_w2, nsa_cv_w1, nsa_cv_w2, nsa_w_out):
    batch, seq, d = x.shape
    n_mixers = 3
    h = x.reshape(batch * seq, d)
    for i in range(norm_g.shape[0]):
        j, kind = divmod(i, n_mixers)
        h = _ffn(h, norm_g[i, 0], ffn1_wi, ffn1_wo, i)
        if kind == 0:
            mixed = _moba_layer(h, norm_g[i, 1], moba_w_in, j, moba_q_gain[j], moba_k_gain[j], moba_w_out,
                                batch=batch, seq=seq)
        elif kind == 1:
            mixed = _lru_layer(h, norm_g[i, 1], lru_w_in, j, lru_conv_w[j], lru_conv_b[j], lru_wa[j], lru_ba[j],
                               lru_wx[j], lru_bx[j], lru_lam[j], lru_w_out, batch=batch, seq=seq)
        else:
            mixed = _nsa_layer(h, norm_g[i, 1], nsa_w_in, j, nsa_gate_b[j], nsa_q_gain[j], nsa_kc_gain[j],
                               nsa_ks_gain[j], nsa_kw_gain[j], nsa_pos_k[j], nsa_pos_v[j], nsa_ck_w1[j], nsa_ck_w2[j],
                               nsa_cv_w1[j], nsa_cv_w2[j], nsa_w_out, batch=batch, seq=seq)
        h = _ffn(h, norm_g[i, 2], ffn2_wi, ffn2_wo, i, mixed=mixed)
    return h.reshape(batch, seq, d)
```

```python
import functools

import jax
import jax.numpy as jnp
from jax import lax
from jax.experimental import pallas as pl
from jax.experimental.pallas import tpu as pltpu

F32 = jnp.float32
BF16 = jnp.bfloat16

EPS = 1e-6
NEG = -1e30
FORCE = 1e30
FFN_RESID = 0.5

MOBA_HEADS = 16
MOBA_BLOCK = 256
MOBA_TOPK = 3

LRU_BLOCKS = 4
CONV_WIDTH = 4
LRU_C = 8.0

NSA_HEADS = 16
NSA_KV_GROUPS = 4
NSA_HEAD_DIM = 64
NSA_HPG = NSA_HEADS // NSA_KV_GROUPS
NSA_CMP_BLOCK = 32
NSA_CMP_STRIDE = 16
NSA_SEL_BLOCK = 64
NSA_TOPN = 16
NSA_LOCAL_BLOCKS = 2
NSA_WINDOW = 512
NSA_N_BRANCH = 3

V7X_VMEM_BYTES = 64 * 1024 * 1024
VMEM_LIMIT_BYTES = V7X_VMEM_BYTES - 8 * 1024 * 1024

LANES = 128
SUBLANES = 8
LOG2E = 1.4426950408889634

ATTN_TILE = 256


def _params(*sem):
    return pltpu.CompilerParams(dimension_semantics=sem, vmem_limit_bytes=VMEM_LIMIT_BYTES)


def _resident(shape):
    nd = len(shape)
    return pl.BlockSpec(shape, lambda *_: (0,) * nd, pipeline_mode=pl.Buffered(1))


def _layer_block(stack, layer):
    nd = stack.ndim
    return pl.BlockSpec((None,) + stack.shape[1:], lambda *_: (layer,) + (0,) * (nd - 1), pipeline_mode=pl.Buffered(1))


def _rms_rows(x, g):
    return x * lax.rsqrt(jnp.mean(x * x, axis=-1, keepdims=True) + EPS) * g


def _gelu_tanh(x):
    c = 0.7978845608028654
    return 0.5 * x * (1.0 + jnp.tanh(c * (x + 0.044715 * (x * x * x))))


def _dot(a, b):
    return jnp.dot(a, b, preferred_element_type=F32)


def _dot_nt(a, b):
    return lax.dot_general(a, b, (((1,), (1,)), ((), ())), preferred_element_type=F32)


def _ffn_kernel(x_ref, *refs, d_ff, chunk, mix):
    if mix is None:
        g_ref, wi_ref, wo_ref, o_ref = refs
        x = x_ref[...]
    else:
        a_ref, wm_ref, g_ref, wi_ref, wo_ref, o_ref = refs
        contract = (((0,), (0,)), ((), ())) if mix == "feature_major" else (((1,), (0,)), ((), ()))
        x = x_ref[...] + lax.dot_general(a_ref[...], wm_ref[...].astype(BF16), contract, preferred_element_type=F32)
    xn = _rms_rows(x, g_ref[...]).astype(BF16)
    acc = jnp.zeros(x.shape, F32)
    for c0 in range(0, d_ff, chunk):
        gate = _dot(xn, wi_ref[:, c0:c0 + chunk].astype(BF16))
        up = _dot(xn, wi_ref[:, d_ff + c0:d_ff + c0 + chunk].astype(BF16))
        act = (gate * jax.nn.sigmoid(gate) * up).astype(BF16)
        acc = acc + _dot(act, wo_ref[c0:c0 + chunk, :].astype(BF16))
    o_ref[...] = x + FFN_RESID * acc


def _ffn(h, g, wi, wo, layer, *, mixed=None, tm=512, chunk=256):
    n, d = h.shape
    d_ff = wo.shape[1]
    row = pl.BlockSpec((tm, d), lambda i: (i, 0))
    operands, specs, mix = [h], [row], None
    if mixed is not None:
        a, w_mix, index, mix = mixed
        k = w_mix.shape[1]
        a_spec = pl.BlockSpec((k, tm), lambda i: (0, i)) if mix == "feature_major" else pl.BlockSpec((tm, k), lambda i: (i, 0))
        operands += [a, w_mix]
        specs += [a_spec, _layer_block(w_mix, index)]
    return pl.pallas_call(
        functools.partial(_ffn_kernel, d_ff=d_ff, chunk=chunk, mix=mix),
        grid=(n // tm,),
        in_specs=specs + [_resident((1, d)), _layer_block(wi, layer), _layer_block(wo, layer)],
        out_specs=row,
        out_shape=jax.ShapeDtypeStruct((n, d), F32),
        compiler_params=_params("parallel"),
        name="ffn",
    )(*operands, g.reshape(1, d), wi, wo)


PROJ_XPOSE_COLS = 512


def _proj_t_kernel(x_ref, g_ref, w_ref, col_ref, *refs, segs, head_dim, n_main):
    if len(refs) == 2:
        (o_ref, wt_ref), tail_ref, aux_ref = refs, None, None
    else:
        tail_ref, o_ref, aux_ref, wt_ref = refs

    @pl.when(pl.program_id(0) == 0)
    def _():
        for c0 in range(0, n_main, PROJ_XPOSE_COLS):
            wt_ref[c0:c0 + PROJ_XPOSE_COLS, :] = w_ref[:, c0:c0 + PROJ_XPOSE_COLS].T.astype(BF16)

    xn = _rms_rows(x_ref[...], g_ref[...]).astype(BF16)
    tm = xn.shape[0]
    for r0, r1, mode in segs:
        acc = _dot_nt(tail_ref[...] if mode == "sigmoid" else wt_ref[r0:r1, :], xn)
        if mode == "raw":
            o_ref[r0:r1, :] = acc.astype(o_ref.dtype)
        elif mode == "norm":
            nh = (r1 - r0) // head_dim
            a3 = acc.reshape(nh, head_dim, tm)
            ms = jnp.mean(a3 * a3, axis=1, keepdims=True)
            y = a3 * lax.rsqrt(ms + EPS) * col_ref[r0:r1, :].reshape(nh, head_dim, 1)
            o_ref[r0:r1, :] = y.reshape(r1 - r0, tm).astype(o_ref.dtype)
        else:
            aux_ref[...] = jax.nn.sigmoid(acc + col_ref[r0:r1, :])


def _proj_t(h, g, w, layer, col, segs, *, head_dim, tm=1024):
    n, d = h.shape
    rows = w.shape[2]
    n_main = max(r1 for _, r1, mode in segs if mode != "sigmoid")
    aux = [(r0, r1) for r0, r1, mode in segs if mode == "sigmoid"]
    assert n_main % PROJ_XPOSE_COLS == 0
    operands = [h, g.reshape(1, d), w, col]
    in_specs = [pl.BlockSpec((tm, d), lambda i: (i, 0)), _resident((1, d)), _layer_block(w, layer), _resident((rows, 1))]
    out_shape = [jax.ShapeDtypeStruct((n_main, n), BF16)]
    out_specs = [pl.BlockSpec((n_main, tm), lambda i: (0, i))]
    if aux:
        ((a0, a1),) = aux
        operands.append(w[layer, :, a0:a1].T.astype(BF16))
        in_specs.append(_resident((a1 - a0, d)))
        out_shape.append(jax.ShapeDtypeStruct((a1 - a0, n), F32))
        out_specs.append(pl.BlockSpec((a1 - a0, tm), lambda i: (0, i)))
    return pl.pallas_call(
        functools.partial(_proj_t_kernel, segs=segs, head_dim=head_dim, n_main=n_main),
        grid=(n // tm,),
        in_specs=in_specs,
        out_specs=out_specs,
        out_shape=out_shape,
        scratch_shapes=[pltpu.VMEM((n_main, d), BF16)],
        compiler_params=_params("arbitrary"),
        name="proj_t",
    )(*operands)


ONES_ROWS = 16
FLASH_LEAD = 3


def _flash_pipelined(ss_ref, n_past, diag_tile, score_fn, value_fn, hd, width):
    n_chain = ss_ref.shape[1]
    n_step = n_past + 1
    last = jnp.maximum(n_past - 1, 0)
    init = tuple((jnp.full((1, width), NEG, F32), jnp.zeros((hd + ONES_ROWS, width), F32)) for _ in range(n_chain))

    def step(i, carries, slot, prefetch):
        nxt = jnp.minimum(i, last)
        cur = jnp.where(i == 0, diag_tile, i - 1)
        if prefetch:
            for c in range(min(FLASH_LEAD, n_chain)):
                ss_ref[1 - slot, c] = score_fn(c, nxt)
        out = []
        for c in range(n_chain):
            s = ss_ref[slot, c]
            m, acc = carries[c]
            m_new = jnp.maximum(m, jnp.max(s, axis=0, keepdims=True))
            p = jnp.exp2(s - m_new).astype(BF16)
            out.append((m_new, jnp.exp2(m - m_new) * acc + _dot(value_fn(c, cur), p)))
            if prefetch and c + FLASH_LEAD < n_chain:
                ss_ref[1 - slot, c + FLASH_LEAD] = score_fn(c + FLASH_LEAD, nxt)
        return tuple(out)

    def two_steps(k, carries):
        return step(2 * k + 1, step(2 * k, carries, 0, True), 1, True)

    carries = lax.fori_loop(0, n_step // 2, two_steps, init)
    return lax.cond(n_step % 2 == 1, lambda c: step(n_step - 1, c, 0, False), lambda c: c, carries)


def _moba_kernel(qt_ref, kt_ref, vt_ref, ot_ref, ka_ref, ss_ref, *, seq, hb, hd):
    t = ATTN_TILE
    nb = seq // t
    kw = ka_ref.shape[-1]
    rows = [slice(h * hd, (h + 1) * hd) for h in range(hb)]
    col = lax.broadcasted_iota(jnp.int32, (t, kw - hd), 1)

    def fill_keys(j, _):
        j0 = pl.multiple_of(j * t, t)
        onehot = jnp.where(col == j, 1.0, 0.0).astype(BF16)
        for h in range(hb):
            ka_ref[h, pl.ds(j0, t), :] = jnp.concatenate([kt_ref[rows[h], pl.ds(j0, t)].T, onehot], axis=1)
        return 0

    lax.fori_loop(0, nb, fill_keys, 0)
    avg = jnp.where(lax.broadcasted_iota(jnp.int32, (nb, seq), 1) // t
                    == lax.broadcasted_iota(jnp.int32, (nb, seq), 0), 1.0 / t, 0.0).astype(BF16)
    km = []
    for h in range(hb):
        kmean = _dot(avg, ka_ref[h])
        km_hi = kmean.astype(BF16)
        km.append((km_hi, (kmean - km_hi.astype(F32)).astype(BF16)))
    blk = lax.broadcasted_iota(jnp.int32, (nb, t), 0)
    causal = (lax.broadcasted_iota(jnp.int32, (t, t), 0) <= lax.broadcasted_iota(jnp.int32, (t, t), 1))
    ones = jnp.ones((ONES_ROWS, t), BF16)

    def q_tile(qi, _):
        q0 = pl.multiple_of(qi * t, t)
        qs = [qt_ref[r, pl.ds(q0, t)] for r in rows]
        qz = [jnp.concatenate([q, jnp.zeros((kw - hd, t), BF16)], axis=0) for q in qs]
        gates = [_dot(km[h][0], qz[h]) + _dot(km[h][1], qz[h]) for h in range(hb)]
        for h in range(hb):
            ss_ref[0, h] = jnp.where(causal, _dot(ka_ref[h, pl.ds(q0, t), :], qz[h]), NEG)
        past = blk < qi
        qa = []
        for h in range(hb):
            gate = jnp.where(past, gates[h], NEG)
            sel = jnp.zeros((nb, t), jnp.bool_)
            for _ in range(min(MOBA_TOPK, nb)):
                top = jnp.max(gate, axis=0, keepdims=True)
                first = jnp.min(jnp.where(gate == top, blk, nb), axis=0, keepdims=True)
                hit = blk == first
                sel = jnp.logical_or(sel, hit)
                gate = jnp.where(hit, -jnp.inf, gate)
            bias = jnp.where(jnp.logical_and(sel, past), 0.0, NEG).astype(BF16)
            qa.append(jnp.concatenate([qs[h], bias, jnp.zeros((kw - hd - nb, t), BF16)], axis=0))

        def score(h, j):
            return _dot(ka_ref[h, pl.ds(pl.multiple_of(j * t, t), t), :], qa[h])

        def value(h, j):
            return jnp.concatenate([vt_ref[rows[h], pl.ds(pl.multiple_of(j * t, t), t)], ones], axis=0)

        carries = _flash_pipelined(ss_ref, qi, qi, score, value, hd, t)
        for h in range(hb):
            _, acc = carries[h]
            ot_ref[rows[h], pl.ds(q0, t)] = (acc[:hd] / acc[hd:hd + 1]).astype(ot_ref.dtype)
        return 0

    lax.fori_loop(0, nb, q_tile, 0)


MOBA_HEADS_PER_STEP = 8


def _moba_attention(qkv_t, *, batch, seq, heads, head_dim):
    n = batch * seq
    hb = MOBA_HEADS_PER_STEP
    hg = heads // hb
    blk = lambda third: pl.BlockSpec((hb * head_dim, seq), lambda b, h: (third * hg + h, b))
    return pl.pallas_call(
        functools.partial(_moba_kernel, seq=seq, hb=hb, hd=head_dim),
        grid=(batch, hg),
        in_specs=[blk(0), blk(1), blk(2)],
        out_specs=blk(0),
        out_shape=jax.ShapeDtypeStruct((heads * head_dim, n), BF16),
        scratch_shapes=[pltpu.VMEM((hb, seq, LANES), BF16), pltpu.VMEM((2, hb, ATTN_TILE, ATTN_TILE), F32)],
        compiler_params=_params("parallel", "parallel"),
        name="moba_attention",
    )(qkv_t, qkv_t, qkv_t)


def _moba_layer(h, g, w_in, j, q_gain, k_gain, w_out, *, batch, seq):
    heads = MOBA_HEADS
    hd = w_in.shape[2] // (3 * heads)
    hw = heads * hd
    assert seq % MOBA_BLOCK == 0 and MOBA_BLOCK == ATTN_TILE and hd + seq // MOBA_BLOCK <= LANES
    col = jnp.concatenate([jnp.tile(q_gain, heads) * (hd ** -0.5 * LOG2E), jnp.tile(k_gain, heads), jnp.ones((hw,), F32)])
    segs = ((0, hw, "norm"), (hw, 2 * hw, "norm"), (2 * hw, 3 * hw, "raw"))
    (qkv_t,) = _proj_t(h, g, w_in, j, col.reshape(-1, 1), segs, head_dim=hd)
    o_t = _moba_attention(qkv_t, batch=batch, seq=seq, heads=heads, head_dim=hd)
    return o_t, w_out, j, "feature_major"


def _lru_proj_kernel(x_ref, g_ref, w_ref, xb_ref, y_ref, wb_ref, *, width):
    @pl.when(pl.program_id(0) == 0)
    def _():
        wb_ref[...] = w_ref[...].astype(BF16)

    xn = _rms_rows(x_ref[...], g_ref[...]).astype(BF16)
    xb_ref[...] = _dot(xn, wb_ref[:, :width])
    y_ref[...] = _gelu_tanh(_dot(xn, wb_ref[:, width:]))


def _lru_proj(h, g, w, layer, *, tm=1024):
    n, d = h.shape
    width = w.shape[2] // 2
    return pl.pallas_call(
        functools.partial(_lru_proj_kernel, width=width),
        grid=(n // tm,),
        in_specs=[pl.BlockSpec((tm, d), lambda i: (i, 0)), _resident((1, d)), _layer_block(w, layer)],
        out_specs=[pl.BlockSpec((tm, width), lambda i: (i, 0))] * 2,
        out_shape=[jax.ShapeDtypeStruct((n, width), F32)] * 2,
        scratch_shapes=[pltpu.VMEM(w.shape[1:], BF16)],
        compiler_params=_params("arbitrary"),
        name="lru_proj",
    )(h, g.reshape(1, d), w)


LRU_CONV_PAD = SUBLANES


def _lru_kernel(xb_ref, y_ref, cw_ref, cb_ref, wa_ref, ba_ref, wx_ref, bx_ref, lam_ref, o_ref,
                xpad, a_s, b_s, h_s, h_carry, *, tile, width):
    si = pl.program_id(1)
    pad = LRU_CONV_PAD

    @pl.when(si == 0)
    def _():
        xpad[...] = jnp.zeros((pad, width), F32)
        h_carry[...] = jnp.zeros((1, width), F32)

    cur = xb_ref[...]
    prev = xpad[...]
    sub = lax.broadcasted_iota(jnp.int32, (pad, width), 0)
    xc = cb_ref[...] + cw_ref[CONV_WIDTH - 1:CONV_WIDTH, :] * cur
    for s in range(1, CONV_WIDTH):
        rolled = pltpu.roll(cur, shift=s, axis=0)
        head = jnp.where(sub < s, pltpu.roll(prev, shift=s, axis=0), rolled[:pad])
        shifted = jnp.concatenate([head, rolled[pad:]], axis=0)
        xc = xc + cw_ref[CONV_WIDTH - 1 - s:CONV_WIDTH - s, :] * shifted
    xcb = xc.astype(BF16)
    first = (lax.broadcasted_iota(jnp.int32, (tile, 1), 0) + si * tile) == 0
    bw = width // LRU_BLOCKS
    for n in range(LRU_BLOCKS):
        sl = slice(n * bw, (n + 1) * bw)
        r = jax.nn.sigmoid(_dot(xcb[:, sl], wa_ref[n]) + ba_ref[:, sl])
        i = jax.nn.sigmoid(_dot(xcb[:, sl], wx_ref[n]) + bx_ref[:, sl])
        z = -lam_ref[:, sl]
        softplus = jnp.maximum(z, 0.0) + jnp.log1p(jnp.exp(-jnp.abs(z)))
        a = jnp.exp(-LRU_C * r * softplus)
        d = 1.0 - a * a
        mult = jnp.where(first, 1.0, jnp.where(d > 0.0, d * lax.rsqrt(d), 0.0))
        a_s[:, sl] = a
        b_s[:, sl] = mult * i * xc[:, sl]

    def step(t, h):
        h = a_s[pl.ds(t, 1), :] * h + b_s[pl.ds(t, 1), :]
        h_s[pl.ds(t, 1), :] = h
        return h

    h_carry[...] = lax.fori_loop(0, tile, step, h_carry[...], unroll=8)
    o_ref[...] = (h_s[...] * y_ref[...]).astype(o_ref.dtype)
    xpad[...] = xb_ref[tile - pad:tile, :]


def _lru_recurrence(xb, y, cw, cb, wa, ba, wx, bx, lam, *, batch, seq, tile=512):
    n, width = xb.shape
    nt = seq // tile
    row = lambda b, s: (b * nt + s, 0)
    vec = _resident((1, width))
    return pl.pallas_call(
        functools.partial(_lru_kernel, tile=tile, width=width),
        grid=(batch, nt),
        in_specs=[
            pl.BlockSpec((tile, width), row),
            pl.BlockSpec((tile, width), row),
            _resident(cw.shape), vec,
            _resident(wa.shape), vec,
            _resident(wx.shape), vec,
            vec,
        ],
        out_specs=pl.BlockSpec((tile, width), row),
        out_shape=jax.ShapeDtypeStruct((n, width), BF16),
        scratch_shapes=[
            pltpu.VMEM((LRU_CONV_PAD, width), F32),
            pltpu.VMEM((tile, width), F32),
            pltpu.VMEM((tile, width), F32),
            pltpu.VMEM((tile, width), F32),
            pltpu.VMEM((1, width), F32),
        ],
        compiler_params=_params("parallel", "arbitrary"),
        name="lru_recurrence",
    )(xb, y, cw, cb.reshape(1, width), wa, ba.reshape(1, width), wx, bx.reshape(1, width), lam.reshape(1, width))


def _lru_layer(h, g, w_in, j, conv_w, conv_b, wa, ba, wx, bx, lam, w_out, *, batch, seq):
    xb, y = _lru_proj(h, g, w_in, j)
    gated = _lru_recurrence(xb, y, conv_w.reshape(CONV_WIDTH, -1), conv_b, wa.astype(BF16), ba, wx.astype(BF16), bx, lam,
                            batch=batch, seq=seq)
    return gated, w_out, j, "token_major"


def _nsa_compress_kernel(kct_ref, vct_ref, pos_ref, w1_ref, kw2_ref, vw2t_ref, kg_ref, ko_ref, vot_ref, tok_ref, *, seq):
    t = ATTN_TILE
    rows = seq // NSA_CMP_STRIDE
    hid = kw2_ref.shape[0]

    def fill_tokens(j, _):
        j0 = pl.multiple_of(j * t, t)
        kv = jnp.concatenate([kct_ref[:, pl.ds(j0, t)], vct_ref[:, pl.ds(j0, t)]], axis=0)
        tok_ref[pl.ds(j0, t), :] = kv.T.astype(F32)
        return 0

    lax.fori_loop(0, seq // t, fill_tokens, 0)
    tok_ref[seq:seq + NSA_CMP_STRIDE, :] = jnp.zeros((NSA_CMP_STRIDE, tok_ref.shape[1]), F32)

    def position(p):
        return (tok_ref[pl.ds(p, rows, stride=NSA_CMP_STRIDE), :] + pos_ref[p:p + 1, :]).astype(BF16)

    acc = jnp.zeros((rows, 2 * hid), F32)
    for p in range(0, NSA_CMP_BLOCK, 2):
        acc = acc + _dot(jnp.concatenate([position(p), position(p + 1)], axis=1), w1_ref[p // 2])
    hidden = _gelu_tanh(acc).astype(BF16)
    ko_ref[...] = _rms_rows(_dot(hidden[:, :hid], kw2_ref[...]), kg_ref[...]).astype(ko_ref.dtype)
    vot_ref[...] = _dot_nt(vw2t_ref[...], hidden[:, hid:]).astype(vot_ref.dtype)


def _nsa_compress(main_t, pos, w1_pairs, kw2, vw2t, k_gain, *, batch, seq, row_kc, row_vc):
    groups, hd = NSA_KV_GROUPS, NSA_HEAD_DIM
    bg, rows = batch * groups, seq // NSA_CMP_STRIDE
    kv_rows = lambda row0: pl.BlockSpec((hd, seq), lambda i: (row0 // hd + i % groups, i // groups))
    return pl.pallas_call(
        functools.partial(_nsa_compress_kernel, seq=seq),
        grid=(bg,),
        in_specs=[kv_rows(row_kc), kv_rows(row_vc), _resident(pos.shape), _resident(w1_pairs.shape), _resident(kw2.shape),
                  _resident(vw2t.shape), _resident((1, hd))],
        out_specs=[pl.BlockSpec((None, rows, hd), lambda i: (i, 0, 0)), pl.BlockSpec((None, hd, rows), lambda i: (i, 0, 0))],
        out_shape=[jax.ShapeDtypeStruct((bg, rows, hd), BF16), jax.ShapeDtypeStruct((bg, hd, rows), BF16)],
        scratch_shapes=[pltpu.VMEM((seq + NSA_CMP_STRIDE, 2 * hd), F32)],
        compiler_params=_params("parallel"),
        name="nsa_compress",
    )(main_t, main_t, pos, w1_pairs, kw2, vw2t, k_gain.reshape(1, hd))


def _nsa_kernel(qt_ref, kc_ref, vct_ref, kst_ref, vst_ref, kwt_ref, vwt_ref, gt_ref, ot_ref, ks_ref, kw_ref, ss_ref,
                *, seq):
    t = ATTN_TILE
    hd = NSA_HEAD_DIM
    ncr = seq // NSA_CMP_STRIDE
    nsl = seq // NSA_SEL_BLOCK
    n_top = min(NSA_TOPN, nsl)
    ones_col = jnp.where(lax.broadcasted_iota(jnp.int32, (t, hd), 1) == 0, 1.0, 0.0).astype(BF16)
    sel_col = lax.broadcasted_iota(jnp.int32, (t, nsl), 1)
    sel_blk = lax.broadcasted_iota(jnp.int32, (t, nsl), 0) // NSA_SEL_BLOCK

    def fill_keys(j, _):
        j0 = pl.multiple_of(j * t, t)
        onehot = jnp.where(sel_col == j * (t // NSA_SEL_BLOCK) + sel_blk, 1.0, 0.0).astype(BF16)
        ks_ref[pl.ds(j0, t), :] = jnp.concatenate([kst_ref[:, pl.ds(j0, t)].T, onehot], axis=1)
        kw_ref[pl.ds(j0, t), :] = jnp.concatenate([kwt_ref[:, pl.ds(j0, t)].T, ones_col], axis=1)
        return 0

    lax.fori_loop(0, seq // t, fill_keys, 0)
    gate_row0 = pl.program_id(1) * (NSA_HPG * NSA_N_BRANCH)
    jn = lax.broadcasted_iota(jnp.int32, (nsl, ncr), 0) * NSA_SEL_BLOCK
    cn = lax.broadcasted_iota(jnp.int32, (nsl, ncr), 1) * NSA_CMP_STRIDE
    ov_t = jnp.where(jnp.logical_and(cn < jn + NSA_SEL_BLOCK, cn + NSA_CMP_BLOCK > jn), 1.0, 0.0).astype(BF16)
    tw = NSA_HPG * t
    diff = lax.broadcasted_iota(jnp.int32, (t, tw), 0) - (lax.broadcasted_iota(jnp.int32, (t, tw), 1) & (t - 1))
    causal = (lax.broadcasted_iota(jnp.int32, (t, t), 0) <= lax.broadcasted_iota(jnp.int32, (t, t), 1))
    cmp_end = lax.broadcasted_iota(jnp.int32, (ncr, tw), 0) * NSA_CMP_STRIDE + (NSA_CMP_BLOCK - 1)
    cmp_q = lax.broadcasted_iota(jnp.int32, (ncr, tw), 1) & (t - 1)
    blk_j = lax.broadcasted_iota(jnp.int32, (nsl, t), 0)
    ones = jnp.ones((ONES_ROWS, t), BF16)

    def q_tile(qi, _):
        q0 = pl.multiple_of(qi * t, t)
        qs = [qt_ref[hp * hd:(hp + 1) * hd, pl.ds(q0, t)] for hp in range(NSA_HPG)]
        q = jnp.concatenate(qs, axis=1)

        k1 = pl.multiple_of(jnp.maximum(qi - 1, 0) * t, t)
        k2 = pl.multiple_of(jnp.maximum(qi - 2, 0) * t, t)
        s_cmp = _dot(kc_ref[...], q)
        row0 = lax.broadcasted_iota(jnp.int32, (hd, tw), 0) == 0
        q_pen = [jnp.concatenate([q, jnp.where(row0, pen, 0.0).astype(BF16)], axis=0)
                 for pen in (0.0, jnp.where(qi >= 1, 0.0, NEG), jnp.where(qi >= 2, 0.0, NEG))]
        s_win = [_dot(kw_ref[pl.ds(k0, t), :], qp) for k0, qp in zip((q0, k1, k2), q_pen)]

        s = jnp.where(cmp_end <= q0 + cmp_q, s_cmp, NEG)
        e = jnp.exp2(s - jnp.max(s, axis=0, keepdims=True))
        den = jnp.sum(e, axis=0, keepdims=True)
        p = e * jnp.where(q0 + cmp_q[0:1, :] >= NSA_CMP_BLOCK - 1, 1.0 / den, 0.0)
        o_cmp = _dot(vct_ref[...], p.astype(BF16))
        p_sum = p[:, 0:t]
        for hp in range(1, NSA_HPG):
            p_sum = p_sum + p[:, hp * t:(hp + 1) * t]
        ps_hi = p_sum.astype(BF16)
        ps_lo = (p_sum - ps_hi.astype(F32)).astype(BF16)
        imp = _dot(ov_t, ps_hi) + _dot(ov_t, ps_lo)
        qpos = q0 + lax.broadcasted_iota(jnp.int32, (nsl, t), 1)
        back = qpos // NSA_SEL_BLOCK - blk_j
        forced = jnp.logical_or(blk_j == 0, jnp.logical_and(back >= 0, back < NSA_LOCAL_BLOCKS))
        valid = blk_j * NSA_SEL_BLOCK <= qpos
        score = jnp.where(valid, jnp.where(forced, FORCE, imp), NEG)
        n_grp = nsl // SUBLANES
        tiles = [score[a * SUBLANES:(a + 1) * SUBLANES] for a in range(n_grp)]
        sub = lax.broadcasted_iota(jnp.int32, (SUBLANES, t), 0)

        def count_group(mt, ranks):
            ranks = list(ranks)
            for r in range(SUBLANES):
                row = tiles[mt][r:r + 1, :]
                for a in range(n_grp):
                    if a < mt:
                        ahead = jnp.where(row > tiles[a], 1, 0)
                    elif a > mt:
                        ahead = jnp.where(row >= tiles[a], 1, 0)
                    else:
                        ahead = jnp.where(sub > r, jnp.where(row >= tiles[a], 1, 0), jnp.where(row > tiles[a], 1, 0))
                    ranks[a] = ranks[a] + ahead
            return tuple(ranks)

        ranks = tuple(jnp.zeros((SUBLANES, t), jnp.int32) for _ in range(n_grp))
        last_grp = (q0 + t - 1) // (NSA_SEL_BLOCK * SUBLANES)
        for mt in range(n_grp):
            ranks = lax.cond(mt <= last_grp, functools.partial(count_group, mt), lambda r: r, ranks)
        sel = jnp.logical_and(jnp.concatenate(ranks, axis=0) < n_top, valid)
        bias = jnp.where(sel, 0.0, NEG).astype(BF16)

        lower = diff <= 0
        s_x = jnp.where(lower, s_win[0], s_win[2])
        m_w = jnp.maximum(jnp.max(s_x, axis=0, keepdims=True), jnp.max(s_win[1], axis=0, keepdims=True))
        p_x = jnp.exp2(s_x - m_w)
        p_win = [jnp.where(lower, p_x, 0.0), jnp.exp2(s_win[1] - m_w), jnp.where(lower, 0.0, p_x)]
        acc_w = functools.reduce(jnp.add, [
            _dot(jnp.concatenate([vwt_ref[:, pl.ds(k0, t)], ones], axis=0), p.astype(BF16))
            for p, k0 in zip(p_win, (q0, k1, k2))])
        o_win = acc_w[:hd] / acc_w[hd:hd + 1]

        qa = [jnp.concatenate([qh, bias], axis=0) for qh in qs]
        for hp in range(NSA_HPG):
            ss_ref[0, hp] = jnp.where(causal, _dot(ks_ref[pl.ds(q0, t), :], qa[hp]), NEG)

        def score(hp, j):
            return _dot(ks_ref[pl.ds(pl.multiple_of(j * t, t), t), :], qa[hp])

        def value(hp, j):
            return jnp.concatenate([vst_ref[:, pl.ds(pl.multiple_of(j * t, t), t)], ones], axis=0)

        carries = _flash_pipelined(ss_ref, qi, qi, score, value, hd, t)
        for hp in range(NSA_HPG):
            _, acc = carries[hp]
            cols = slice(hp * t, (hp + 1) * t)
            gate = [gt_ref[pl.ds(gate_row0 + NSA_N_BRANCH * hp + br, 1), pl.ds(q0, t)] for br in range(NSA_N_BRANCH)]
            o = gate[0] * o_cmp[:, cols] + gate[1] * (acc[:hd] / acc[hd:hd + 1]) + gate[2] * o_win[:, cols]
            ot_ref[hp * hd:(hp + 1) * hd, pl.ds(q0, t)] = o.astype(ot_ref.dtype)
        return 0

    lax.fori_loop(0, seq // t, q_tile, 0)


def _nsa_attention(main_t, k_cmp, v_cmp_t, gates_t, *, batch, seq, row_ks, row_vs, row_kw, row_vw):
    groups, hd, hpg = NSA_KV_GROUPS, NSA_HEAD_DIM, NSA_HPG
    n = batch * seq
    ncr = seq // NSA_CMP_STRIDE
    nsl = seq // NSA_SEL_BLOCK
    gh = hpg * hd
    kv_rows = lambda row0: pl.BlockSpec((hd, seq), lambda b, g: (row0 // hd + g, b))
    return pl.pallas_call(
        functools.partial(_nsa_kernel, seq=seq),
        grid=(batch, groups),
        in_specs=[
            pl.BlockSpec((gh, seq), lambda b, g: (g, b)),
            pl.BlockSpec((None, ncr, hd), lambda b, g: (b * groups + g, 0, 0)),
            pl.BlockSpec((None, hd, ncr), lambda b, g: (b * groups + g, 0, 0)),
            kv_rows(row_ks), kv_rows(row_vs), kv_rows(row_kw), kv_rows(row_vw),
            pl.BlockSpec((gates_t.shape[0], seq), lambda b, g: (0, b)),
        ],
        out_specs=pl.BlockSpec((gh, seq), lambda b, g: (g, b)),
        out_shape=jax.ShapeDtypeStruct((groups * gh, n), BF16),
        scratch_shapes=[pltpu.VMEM((seq, hd + nsl), BF16), pltpu.VMEM((seq, 2 * hd), BF16),
                        pltpu.VMEM((2, hpg, ATTN_TILE, ATTN_TILE), F32)],
        compiler_params=_params("parallel", "parallel"),
        name="nsa_attention",
    )(main_t, k_cmp, v_cmp_t, main_t, main_t, main_t, main_t, gates_t)


def _nsa_layer(h, g, w_in, j, gate_b, q_gain, kc_gain, ks_gain, kw_gain, pos_k, pos_v, ck_w1, ck_w2, cv_w1, cv_w2, w_out,
               *, batch, seq):
    heads, groups, hd, hpg = NSA_HEADS, NSA_KV_GROUPS, NSA_HEAD_DIM, NSA_HPG
    assert seq % ATTN_TILE == 0 and NSA_WINDOW == 2 * ATTN_TILE and NSA_CMP_BLOCK == 2 * NSA_CMP_STRIDE
    qw, kvw = heads * hd, groups * hd
    r_kc, r_vc, r_ks, r_vs, r_kw, r_vw, r_g = (qw + i * kvw for i in range(7))
    n_gate = NSA_N_BRANCH * heads
    ones = jnp.ones((kvw,), F32)
    col = jnp.concatenate([jnp.tile(q_gain, heads) * (hd ** -0.5 * LOG2E), ones, ones, jnp.tile(ks_gain, groups), ones,
                           jnp.tile(kw_gain, groups), ones, gate_b])
    segs = ((0, qw, "norm"), (r_kc, r_ks, "raw"), (r_ks, r_vs, "norm"), (r_vs, r_kw, "raw"), (r_kw, r_vw, "norm"),
            (r_vw, r_g, "raw"), (r_g, r_g + n_gate, "sigmoid"))
    main_t, gates_t = _proj_t(h, g, w_in, j, col.reshape(-1, 1), segs, head_dim=hd)

    l, hid = NSA_CMP_BLOCK, ck_w1.shape[1]
    zero = jnp.zeros((l, hd, hid), F32)
    w1 = jnp.concatenate([jnp.concatenate([ck_w1.reshape(l, hd, hid), zero], axis=2),
                          jnp.concatenate([zero, cv_w1.reshape(l, hd, hid)], axis=2)], axis=1)
    w1_pairs = w1.reshape(l // 2, 4 * hd, 2 * hid).astype(BF16)
    k_cmp, v_cmp_t = _nsa_compress(main_t, jnp.concatenate([pos_k, pos_v], axis=1), w1_pairs, ck_w2.astype(BF16),
                                   cv_w2.T.astype(BF16), kc_gain, batch=batch, seq=seq, row_kc=r_kc, row_vc=r_vc)
    o_t = _nsa_attention(main_t, k_cmp, v_cmp_t, gates_t, batch=batch, seq=seq,
                         row_ks=r_ks, row_vs=r_vs, row_kw=r_kw, row_vw=r_vw)
    return o_t, w_out, j, "feature_major"


def kernel(x, norm_g, ffn1_wi, ffn1_wo, ffn2_wi, ffn2_wo, moba_w_in, moba_q_gain, moba_k_gain, moba_w_out, lru_w_in, lru_conv_w, lru_conv_b, lru_wa, lru_ba, lru_wx, lru_bx, lru_lam, lru_w_out, nsa_w_in, nsa_gate_b, nsa_q_gain, nsa_kc_gain, nsa_ks_gain, nsa_kw_gain, nsa_pos_k, nsa_pos_v, nsa_ck_w1, nsa_ck_w2, nsa_cv_w1, nsa_cv_w2, nsa_w_out):
    batch, seq, d = x.shape
    n_mixers = 3
    h = x.reshape(batch * seq, d)
    for i in range(norm_g.shape[0]):
        j, kind = divmod(i, n_mixers)
        h = _ffn(h, norm_g[i, 0], ffn1_wi, ffn1_wo, i)
        if kind == 0:
            mixed = _moba_layer(h, norm_g[i, 1], moba_w_in, j, moba_q_gain[j], moba_k_gain[j], moba_w_out,
                                batch=batch, seq=seq)
        elif kind == 1:
            mixed = _lru_layer(h, norm_g[i, 1], lru_w_in, j, lru_conv_w[j], lru_conv_b[j], lru_wa[j], lru_ba[j],
                               lru_wx[j], lru_bx[j], lru_lam[j], lru_w_out, batch=batch, seq=seq)
        else:
            mixed = _nsa_layer(h, norm_g[i, 1], nsa_w_in, j, nsa_gate_b[j], nsa_q_gain[j], nsa_kc_gain[j],
                               nsa_ks_gain[j], nsa_kw_gain[j], nsa_pos_k[j], nsa_pos_v[j], nsa_ck_w1[j], nsa_ck_w2[j],
                               nsa_cv_w1[j], nsa_cv_w2[j], nsa_w_out, batch=batch, seq=seq)
        h = _ffn(h, norm_g[i, 2], ffn2_wi, ffn2_wo, i, mixed=mixed)
    return h.reshape(batch, seq, d)
```

```python
import functools

import jax
import jax.numpy as jnp
from jax import lax
from jax.experimental import pallas as pl
from jax.experimental.pallas import tpu as pltpu

F32 = jnp.float32
BF16 = jnp.bfloat16

EPS = 1e-6
NEG = -1e30
FORCE = 1e30
FFN_RESID = 0.5

MOBA_HEADS = 16
MOBA_BLOCK = 256
MOBA_TOPK = 3

LRU_BLOCKS = 4
CONV_WIDTH = 4
LRU_C = 8.0

NSA_HEADS = 16
NSA_KV_GROUPS = 4
NSA_HEAD_DIM = 64
NSA_HPG = NSA_HEADS // NSA_KV_GROUPS
NSA_CMP_BLOCK = 32
NSA_CMP_STRIDE = 16
NSA_SEL_BLOCK = 64
NSA_TOPN = 16
NSA_LOCAL_BLOCKS = 2
NSA_WINDOW = 512
NSA_N_BRANCH = 3

V7X_VMEM_BYTES = 64 * 1024 * 1024
VMEM_LIMIT_BYTES = V7X_VMEM_BYTES - 8 * 1024 * 1024

LANES = 128
SUBLANES = 8
LOG2E = 1.4426950408889634

ATTN_TILE = 256


def _params(*sem):
    return pltpu.CompilerParams(dimension_semantics=sem, vmem_limit_bytes=VMEM_LIMIT_BYTES)


def _resident(shape):
    nd = len(shape)
    return pl.BlockSpec(shape, lambda *_: (0,) * nd, pipeline_mode=pl.Buffered(1))


def _layer_block(stack, layer):
    nd = stack.ndim
    return pl.BlockSpec((None,) + stack.shape[1:], lambda *_: (layer,) + (0,) * (nd - 1), pipeline_mode=pl.Buffered(1))


def _rms_rows(x, g):
    return x * lax.rsqrt(jnp.mean(x * x, axis=-1, keepdims=True) + EPS) * g


def _gelu_tanh(x):
    c = 0.7978845608028654
    return 0.5 * x * (1.0 + jnp.tanh(c * (x + 0.044715 * (x * x * x))))


def _dot(a, b):
    return jnp.dot(a, b, preferred_element_type=F32)


def _dot_nt(a, b):
    return lax.dot_general(a, b, (((1,), (1,)), ((), ())), preferred_element_type=F32)


def _ffn_kernel(x_ref, *refs, d_ff, chunk, mix):
    if mix is None:
        g_ref, wi_ref, wo_ref, o_ref = refs
        x = x_ref[...]
    else:
        a_ref, wm_ref, g_ref, wi_ref, wo_ref, o_ref = refs
        contract = (((0,), (0,)), ((), ())) if mix == "feature_major" else (((1,), (0,)), ((), ()))
        x = x_ref[...] + lax.dot_general(a_ref[...], wm_ref[...].astype(BF16), contract, preferred_element_type=F32)
    xn = _rms_rows(x, g_ref[...]).astype(BF16)
    acc = jnp.zeros(x.shape, F32)
    for c0 in range(0, d_ff, chunk):
        gate = _dot(xn, wi_ref[:, c0:c0 + chunk].astype(BF16))
        up = _dot(xn, wi_ref[:, d_ff + c0:d_ff + c0 + chunk].astype(BF16))
        act = (gate * jax.nn.sigmoid(gate) * up).astype(BF16)
        acc = acc + _dot(act, wo_ref[c0:c0 + chunk, :].astype(BF16))
    o_ref[...] = x + FFN_RESID * acc


def _ffn(h, g, wi, wo, layer, *, mixed=None, tm=512, chunk=256):
    n, d = h.shape
    d_ff = wo.shape[1]
    row = pl.BlockSpec((tm, d), lambda i: (i, 0))
    operands, specs, mix = [h], [row], None
    if mixed is not None:
        a, w_mix, index, mix = mixed
        k = w_mix.shape[1]
        a_spec = pl.BlockSpec((k, tm), lambda i: (0, i)) if mix == "feature_major" else pl.BlockSpec((tm, k), lambda i: (i, 0))
        operands += [a, w_mix]
        specs += [a_spec, _layer_block(w_mix, index)]
    return pl.pallas_call(
        functools.partial(_ffn_kernel, d_ff=d_ff, chunk=chunk, mix=mix),
        grid=(n // tm,),
        in_specs=specs + [_resident((1, d)), _layer_block(wi, layer), _layer_block(wo, layer)],
        out_specs=row,
        out_shape=jax.ShapeDtypeStruct((n, d), F32),
        compiler_params=_params("parallel"),
        name="ffn",
    )(*operands, g.reshape(1, d), wi, wo)


PROJ_XPOSE_COLS = 512


def _proj_t_kernel(x_ref, g_ref, w_ref, col_ref, *refs, segs, head_dim, n_main):
    if len(refs) == 2:
        (o_ref, wt_ref), tail_ref, aux_ref = refs, None, None
    else:
        tail_ref, o_ref, aux_ref, wt_ref = refs

    @pl.when(pl.program_id(0) == 0)
    def _():
        for c0 in range(0, n_main, PROJ_XPOSE_COLS):
            wt_ref[c0:c0 + PROJ_XPOSE_COLS, :] = w_ref[:, c0:c0 + PROJ_XPOSE_COLS].T.astype(BF16)

    xn = _rms_rows(x_ref[...], g_ref[...]).astype(BF16)
    tm = xn.shape[0]
    for r0, r1, mode in segs:
        acc = _dot_nt(tail_ref[...] if mode == "sigmoid" else wt_ref[r0:r1, :], xn)
        if mode == "raw":
            o_ref[r0:r1, :] = acc.astype(o_ref.dtype)
        elif mode == "norm":
            nh = (r1 - r0) // head_dim
            a3 = acc.reshape(nh, head_dim, tm)
            ms = jnp.mean(a3 * a3, axis=1, keepdims=True)
            y = a3 * lax.rsqrt(ms + EPS) * col_ref[r0:r1, :].reshape(nh, head_dim, 1)
            o_ref[r0:r1, :] = y.reshape(r1 - r0, tm).astype(o_ref.dtype)
        else:
            aux_ref[...] = jax.nn.sigmoid(acc + col_ref[r0:r1, :])


def _proj_t(h, g, w, layer, col, segs, *, head_dim, tm=1024):
    n, d = h.shape
    rows = w.shape[2]
    n_main = max(r1 for _, r1, mode in segs if mode != "sigmoid")
    aux = [(r0, r1) for r0, r1, mode in segs if mode == "sigmoid"]
    assert n_main % PROJ_XPOSE_COLS == 0
    operands = [h, g.reshape(1, d), w, col]
    in_specs = [pl.BlockSpec((tm, d), lambda i: (i, 0)), _resident((1, d)), _layer_block(w, layer), _resident((rows, 1))]
    out_shape = [jax.ShapeDtypeStruct((n_main, n), BF16)]
    out_specs = [pl.BlockSpec((n_main, tm), lambda i: (0, i))]
    if aux:
        ((a0, a1),) = aux
        operands.append(w[layer, :, a0:a1].T.astype(BF16))
        in_specs.append(_resident((a1 - a0, d)))
        out_shape.append(jax.ShapeDtypeStruct((a1 - a0, n), F32))
        out_specs.append(pl.BlockSpec((a1 - a0, tm), lambda i: (0, i)))
    return pl.pallas_call(
        functools.partial(_proj_t_kernel, segs=segs, head_dim=head_dim, n_main=n_main),
        grid=(n // tm,),
        in_specs=in_specs,
        out_specs=out_specs,
        out_shape=out_shape,
        scratch_shapes=[pltpu.VMEM((n_main, d), BF16)],
        compiler_params=_params("arbitrary"),
        name="proj_t",
    )(*operands)


ONES_ROWS = 16
FLASH_LEAD = 3


def _flash_pipelined(ss_ref, n_past, diag_tile, score_fn, value_fn, hd, width):
    n_chain = ss_ref.shape[1]
    n_step = n_past + 1
    last = jnp.maximum(n_past - 1, 0)
    init = tuple((jnp.full((1, width), NEG, F32), jnp.zeros((hd + ONES_ROWS, width), F32)) for _ in range(n_chain))

    def step(i, carries, slot, prefetch):
        nxt = jnp.minimum(i, last)
        cur = jnp.where(i == 0, diag_tile, i - 1)
        if prefetch:
            for c in range(min(FLASH_LEAD, n_chain)):
                ss_ref[1 - slot, c] = score_fn(c, nxt)
        out = []
        for c in range(n_chain):
            s = ss_ref[slot, c]
            m, acc = carries[c]
            m_new = jnp.maximum(m, jnp.max(s, axis=0, keepdims=True))
            p = jnp.exp2(s - m_new).astype(BF16)
            out.append((m_new, jnp.exp2(m - m_new) * acc + _dot(value_fn(c, cur), p)))
            if prefetch and c + FLASH_LEAD < n_chain:
                ss_ref[1 - slot, c + FLASH_LEAD] = score_fn(c + FLASH_LEAD, nxt)
        return tuple(out)

    def two_steps(k, carries):
        return step(2 * k + 1, step(2 * k, carries, 0, True), 1, True)

    carries = lax.fori_loop(0, n_step // 2, two_steps, init)
    return lax.cond(n_step % 2 == 1, lambda c: step(n_step - 1, c, 0, False), lambda c: c, carries)


def _moba_kernel(qt_ref, kt_ref, vt_ref, ot_ref, ka_ref, ss_ref, *, seq, hb, hd):
    t = ATTN_TILE
    nb = seq // t
    kw = ka_ref.shape[-1]
    rows = [slice(h * hd, (h + 1) * hd) for h in range(hb)]
    col = lax.broadcasted_iota(jnp.int32, (t, kw - hd), 1)

    def fill_keys(j, _):
        j0 = pl.multiple_of(j * t, t)
        onehot = jnp.where(col == j, 1.0, 0.0).astype(BF16)
        for h in range(hb):
            ka_ref[h, pl.ds(j0, t), :] = jnp.concatenate([kt_ref[rows[h], pl.ds(j0, t)].T, onehot], axis=1)
        return 0

    lax.fori_loop(0, nb, fill_keys, 0)
    avg = jnp.where(lax.broadcasted_iota(jnp.int32, (nb, seq), 1) // t
                    == lax.broadcasted_iota(jnp.int32, (nb, seq), 0), 1.0 / t, 0.0).astype(BF16)
    km = []
    for h in range(hb):
        kmean = _dot(avg, ka_ref[h])
        km_hi = kmean.astype(BF16)
        km.append((km_hi, (kmean - km_hi.astype(F32)).astype(BF16)))
    blk = lax.broadcasted_iota(jnp.int32, (nb, t), 0)
    causal = (lax.broadcasted_iota(jnp.int32, (t, t), 0) <= lax.broadcasted_iota(jnp.int32, (t, t), 1))
    ones = jnp.ones((ONES_ROWS, t), BF16)

    def q_tile(qi, _):
        q0 = pl.multiple_of(qi * t, t)
        qs = [qt_ref[r, pl.ds(q0, t)] for r in rows]
        qz = [jnp.concatenate([q, jnp.zeros((kw - hd, t), BF16)], axis=0) for q in qs]
        gates = [_dot(km[h][0], qz[h]) + _dot(km[h][1], qz[h]) for h in range(hb)]
        for h in range(hb):
            ss_ref[0, h] = jnp.where(causal, _dot(ka_ref[h, pl.ds(q0, t), :], qz[h]), NEG)
        past = blk < qi
        qa = []
        for h in range(hb):
            gate = jnp.where(past, gates[h], NEG)
            sel = jnp.zeros((nb, t), jnp.bool_)
            for _ in range(min(MOBA_TOPK, nb)):
                top = jnp.max(gate, axis=0, keepdims=True)
                first = jnp.min(jnp.where(gate == top, blk, nb), axis=0, keepdims=True)
                hit = blk == first
                sel = jnp.logical_or(sel, hit)
                gate = jnp.where(hit, -jnp.inf, gate)
            bias = jnp.where(jnp.logical_and(sel, past), 0.0, NEG).astype(BF16)
            qa.append(jnp.concatenate([qs[h], bias, jnp.zeros((kw - hd - nb, t), BF16)], axis=0))

        def score(h, j):
            return _dot(ka_ref[h, pl.ds(pl.multiple_of(j * t, t), t), :], qa[h])

        def value(h, j):
            return jnp.concatenate([vt_ref[rows[h], pl.ds(pl.multiple_of(j * t, t), t)], ones], axis=0)

        carries = _flash_pipelined(ss_ref, qi, qi, score, value, hd, t)
        for h in range(hb):
            _, acc = carries[h]
            ot_ref[rows[h], pl.ds(q0, t)] = (acc[:hd] / acc[hd:hd + 1]).astype(ot_ref.dtype)
        return 0

    lax.fori_loop(0, nb, q_tile, 0)


MOBA_HEADS_PER_STEP = 8


def _moba_attention(qkv_t, *, batch, seq, heads, head_dim):
    n = batch * seq
    hb = MOBA_HEADS_PER_STEP
    hg = heads // hb
    blk = lambda third: pl.BlockSpec((hb * head_dim, seq), lambda b, h: (third * hg + h, b))
    return pl.pallas_call(
        functools.partial(_moba_kernel, seq=seq, hb=hb, hd=head_dim),
        grid=(batch, hg),
        in_specs=[blk(0), blk(1), blk(2)],
        out_specs=blk(0),
        out_shape=jax.ShapeDtypeStruct((heads * head_dim, n), BF16),
        scratch_shapes=[pltpu.VMEM((hb, seq, LANES), BF16), pltpu.VMEM((2, hb, ATTN_TILE, ATTN_TILE), F32)],
        compiler_params=_params("parallel", "parallel"),
        name="moba_attention",
    )(qkv_t, qkv_t, qkv_t)


def _moba_layer(h, g, w_in, j, q_gain, k_gain, w_out, *, batch, seq):
    heads = MOBA_HEADS
    hd = w_in.shape[2] // (3 * heads)
    hw = heads * hd
    assert seq % MOBA_BLOCK == 0 and MOBA_BLOCK == ATTN_TILE and hd + seq // MOBA_BLOCK <= LANES
    col = jnp.concatenate([jnp.tile(q_gain, heads) * (hd ** -0.5 * LOG2E), jnp.tile(k_gain, heads), jnp.ones((hw,), F32)])
    segs = ((0, hw, "norm"), (hw, 2 * hw, "norm"), (2 * hw, 3 * hw, "raw"))
    (qkv_t,) = _proj_t(h, g, w_in, j, col.reshape(-1, 1), segs, head_dim=hd)
    o_t = _moba_attention(qkv_t, batch=batch, seq=seq, heads=heads, head_dim=hd)
    return o_t, w_out, j, "feature_major"


def _lru_proj_kernel(x_ref, g_ref, w_ref, xb_ref, y_ref, wb_ref, *, width):
    @pl.when(pl.program_id(0) == 0)
    def _():
        wb_ref[...] = w_ref[...].astype(BF16)

    xn = _rms_rows(x_ref[...], g_ref[...]).astype(BF16)
    xb_ref[...] = _dot(xn, wb_ref[:, :width])
    y_ref[...] = _gelu_tanh(_dot(xn, wb_ref[:, width:]))


def _lru_proj(h, g, w, layer, *, tm=1024):
    n, d = h.shape
    width = w.shape[2] // 2
    return pl.pallas_call(
        functools.partial(_lru_proj_kernel, width=width),
        grid=(n // tm,),
        in_specs=[pl.BlockSpec((tm, d), lambda i: (i, 0)), _resident((1, d)), _layer_block(w, layer)],
        out_specs=[pl.BlockSpec((tm, width), lambda i: (i, 0))] * 2,
        out_shape=[jax.ShapeDtypeStruct((n, width), F32)] * 2,
        scratch_shapes=[pltpu.VMEM(w.shape[1:], BF16)],
        compiler_params=_params("arbitrary"),
        name="lru_proj",
    )(h, g.reshape(1, d), w)


LRU_CONV_PAD = SUBLANES


def _lru_kernel(xb_ref, y_ref, cw_ref, cb_ref, wa_ref, ba_ref, wx_ref, bx_ref, lam_ref, o_ref,
                xpad, a_s, b_s, h_s, h_carry, *, tile, width):
    si = pl.program_id(1)
    pad = LRU_CONV_PAD

    @pl.when(si == 0)
    def _():
        xpad[...] = jnp.zeros((pad, width), F32)
        h_carry[...] = jnp.zeros((1, width), F32)

    cur = xb_ref[...]
    prev = xpad[...]
    sub = lax.broadcasted_iota(jnp.int32, (pad, width), 0)
    xc = cb_ref[...] + cw_ref[CONV_WIDTH - 1:CONV_WIDTH, :] * cur
    for s in range(1, CONV_WIDTH):
        rolled = pltpu.roll(cur, shift=s, axis=0)
        head = jnp.where(sub < s, pltpu.roll(prev, shift=s, axis=0), rolled[:pad])
        shifted = jnp.concatenate([head, rolled[pad:]], axis=0)
        xc = xc + cw_ref[CONV_WIDTH - 1 - s:CONV_WIDTH - s, :] * shifted
    xcb = xc.astype(BF16)
    first = (lax.broadcasted_iota(jnp.int32, (tile, 1), 0) + si * tile) == 0
    bw = width // LRU_BLOCKS
    for n in range(LRU_BLOCKS):
        sl = slice(n * bw, (n + 1) * bw)
        r = jax.nn.sigmoid(_dot(xcb[:, sl], wa_ref[n]) + ba_ref[:, sl])
        i = jax.nn.sigmoid(_dot(xcb[:, sl], wx_ref[n]) + bx_ref[:, sl])
        z = -lam_ref[:, sl]
        softplus = jnp.maximum(z, 0.0) + jnp.log1p(jnp.exp(-jnp.abs(z)))
        a = jnp.exp(-LRU_C * r * softplus)
        d = 1.0 - a * a
        mult = jnp.where(first, 1.0, jnp.where(d > 0.0, d * lax.rsqrt(d), 0.0))
        a_s[:, sl] = a
        b_s[:, sl] = mult * i * xc[:, sl]

    def step(t, h):
        h = a_s[pl.ds(t, 1), :] * h + b_s[pl.ds(t, 1), :]
        h_s[pl.ds(t, 1), :] = h
        return h

    h_carry[...] = lax.fori_loop(0, tile, step, h_carry[...], unroll=16)
    o_ref[...] = (h_s[...] * y_ref[...]).astype(o_ref.dtype)
    xpad[...] = xb_ref[tile - pad:tile, :]


def _lru_recurrence(xb, y, cw, cb, wa, ba, wx, bx, lam, *, batch, seq, tile=512):
    n, width = xb.shape
    nt = seq // tile
    row = lambda b, s: (b * nt + s, 0)
    vec = _resident((1, width))
    return pl.pallas_call(
        functools.partial(_lru_kernel, tile=tile, width=width),
        grid=(batch, nt),
        in_specs=[
            pl.BlockSpec((tile, width), row),
            pl.BlockSpec((tile, width), row),
            _resident(cw.shape), vec,
            _resident(wa.shape), vec,
            _resident(wx.shape), vec,
            vec,
        ],
        out_specs=pl.BlockSpec((tile, width), row),
        out_shape=jax.ShapeDtypeStruct((n, width), BF16),
        scratch_shapes=[
            pltpu.VMEM((LRU_CONV_PAD, width), F32),
            pltpu.VMEM((tile, width), F32),
            pltpu.VMEM((tile, width), F32),
            pltpu.VMEM((tile, width), F32),
            pltpu.VMEM((1, width), F32),
        ],
        compiler_params=_params("parallel", "arbitrary"),
        name="lru_recurrence",
    )(xb, y, cw, cb.reshape(1, width), wa, ba.reshape(1, width), wx, bx.reshape(1, width), lam.reshape(1, width))


def _lru_layer(h, g, w_in, j, conv_w, conv_b, wa, ba, wx, bx, lam, w_out, *, batch, seq):
    xb, y = _lru_proj(h, g, w_in, j)
    gated = _lru_recurrence(xb, y, conv_w.reshape(CONV_WIDTH, -1), conv_b, wa.astype(BF16), ba, wx.astype(BF16), bx, lam,
                            batch=batch, seq=seq)
    return gated, w_out, j, "token_major"


def _nsa_compress_kernel(kct_ref, vct_ref, pos_ref, w1_ref, kw2_ref, vw2t_ref, kg_ref, ko_ref, vot_ref, tok_ref, *, seq):
    t = ATTN_TILE
    rows = seq // NSA_CMP_STRIDE
    hid = kw2_ref.shape[0]

    def fill_tokens(j, _):
        j0 = pl.multiple_of(j * t, t)
        kv = jnp.concatenate([kct_ref[:, pl.ds(j0, t)], vct_ref[:, pl.ds(j0, t)]], axis=0)
        tok_ref[pl.ds(j0, t), :] = kv.T.astype(F32)
        return 0

    lax.fori_loop(0, seq // t, fill_tokens, 0)
    tok_ref[seq:seq + NSA_CMP_STRIDE, :] = jnp.zeros((NSA_CMP_STRIDE, tok_ref.shape[1]), F32)

    def position(p):
        return (tok_ref[pl.ds(p, rows, stride=NSA_CMP_STRIDE), :] + pos_ref[p:p + 1, :]).astype(BF16)

    acc = jnp.zeros((rows, 2 * hid), F32)
    for p in range(0, NSA_CMP_BLOCK, 2):
        acc = acc + _dot(jnp.concatenate([position(p), position(p + 1)], axis=1), w1_ref[p // 2])
    hidden = _gelu_tanh(acc).astype(BF16)
    ko_ref[...] = _rms_rows(_dot(hidden[:, :hid], kw2_ref[...]), kg_ref[...]).astype(ko_ref.dtype)
    vot_ref[...] = _dot_nt(vw2t_ref[...], hidden[:, hid:]).astype(vot_ref.dtype)


def _nsa_compress(main_t, pos, w1_pairs, kw2, vw2t, k_gain, *, batch, seq, row_kc, row_vc):
    groups, hd = NSA_KV_GROUPS, NSA_HEAD_DIM
    bg, rows = batch * groups, seq // NSA_CMP_STRIDE
    kv_rows = lambda row0: pl.BlockSpec((hd, seq), lambda i: (row0 // hd + i % groups, i // groups))
    return pl.pallas_call(
        functools.partial(_nsa_compress_kernel, seq=seq),
        grid=(bg,),
        in_specs=[kv_rows(row_kc), kv_rows(row_vc), _resident(pos.shape), _resident(w1_pairs.shape), _resident(kw2.shape),
                  _resident(vw2t.shape), _resident((1, hd))],
        out_specs=[pl.BlockSpec((None, rows, hd), lambda i: (i, 0, 0)), pl.BlockSpec((None, hd, rows), lambda i: (i, 0, 0))],
        out_shape=[jax.ShapeDtypeStruct((bg, rows, hd), BF16), jax.ShapeDtypeStruct((bg, hd, rows), BF16)],
        scratch_shapes=[pltpu.VMEM((seq + NSA_CMP_STRIDE, 2 * hd), F32)],
        compiler_params=_params("parallel"),
        name="nsa_compress",
    )(main_t, main_t, pos, w1_pairs, kw2, vw2t, k_gain.reshape(1, hd))


def _nsa_kernel(qt_ref, kc_ref, vct_ref, kst_ref, vst_ref, kwt_ref, vwt_ref, gt_ref, ot_ref, ks_ref, kw_ref, ss_ref,
                *, seq):
    t = ATTN_TILE
    hd = NSA_HEAD_DIM
    ncr = seq // NSA_CMP_STRIDE
    nsl = seq // NSA_SEL_BLOCK
    n_top = min(NSA_TOPN, nsl)
    ones_col = jnp.where(lax.broadcasted_iota(jnp.int32, (t, hd), 1) == 0, 1.0, 0.0).astype(BF16)
    sel_col = lax.broadcasted_iota(jnp.int32, (t, nsl), 1)
    sel_blk = lax.broadcasted_iota(jnp.int32, (t, nsl), 0) // NSA_SEL_BLOCK

    def fill_keys(j, _):
        j0 = pl.multiple_of(j * t, t)
        onehot = jnp.where(sel_col == j * (t // NSA_SEL_BLOCK) + sel_blk, 1.0, 0.0).astype(BF16)
        ks_ref[pl.ds(j0, t), :] = jnp.concatenate([kst_ref[:, pl.ds(j0, t)].T, onehot], axis=1)
        kw_ref[pl.ds(j0, t), :] = jnp.concatenate([kwt_ref[:, pl.ds(j0, t)].T, ones_col], axis=1)
        return 0

    lax.fori_loop(0, seq // t, fill_keys, 0)
    gate_row0 = pl.program_id(1) * (NSA_HPG * NSA_N_BRANCH)
    jn = lax.broadcasted_iota(jnp.int32, (nsl, ncr), 0) * NSA_SEL_BLOCK
    cn = lax.broadcasted_iota(jnp.int32, (nsl, ncr), 1) * NSA_CMP_STRIDE
    ov_t = jnp.where(jnp.logical_and(cn < jn + NSA_SEL_BLOCK, cn + NSA_CMP_BLOCK > jn), 1.0, 0.0).astype(BF16)
    tw = NSA_HPG * t
    diff = lax.broadcasted_iota(jnp.int32, (t, tw), 0) - (lax.broadcasted_iota(jnp.int32, (t, tw), 1) & (t - 1))
    causal = (lax.broadcasted_iota(jnp.int32, (t, t), 0) <= lax.broadcasted_iota(jnp.int32, (t, t), 1))
    cmp_end = lax.broadcasted_iota(jnp.int32, (ncr, tw), 0) * NSA_CMP_STRIDE + (NSA_CMP_BLOCK - 1)
    cmp_q = lax.broadcasted_iota(jnp.int32, (ncr, tw), 1) & (t - 1)
    blk_j = lax.broadcasted_iota(jnp.int32, (nsl, t), 0)
    ones = jnp.ones((ONES_ROWS, t), BF16)

    def q_tile(qi, _):
        q0 = pl.multiple_of(qi * t, t)
        qs = [qt_ref[hp * hd:(hp + 1) * hd, pl.ds(q0, t)] for hp in range(NSA_HPG)]
        q = jnp.concatenate(qs, axis=1)

        k1 = pl.multiple_of(jnp.maximum(qi - 1, 0) * t, t)
        k2 = pl.multiple_of(jnp.maximum(qi - 2, 0) * t, t)
        s_cmp = _dot(kc_ref[...], q)
        row0 = lax.broadcasted_iota(jnp.int32, (hd, tw), 0) == 0
        q_pen = [jnp.concatenate([q, jnp.where(row0, pen, 0.0).astype(BF16)], axis=0)
                 for pen in (0.0, jnp.where(qi >= 1, 0.0, NEG), jnp.where(qi >= 2, 0.0, NEG))]
        s_win = [_dot(kw_ref[pl.ds(k0, t), :], qp) for k0, qp in zip((q0, k1, k2), q_pen)]

        s = jnp.where(cmp_end <= q0 + cmp_q, s_cmp, NEG)
        e = jnp.exp2(s - jnp.max(s, axis=0, keepdims=True))
        den = jnp.sum(e, axis=0, keepdims=True)
        p = e * jnp.where(q0 + cmp_q[0:1, :] >= NSA_CMP_BLOCK - 1, 1.0 / den, 0.0)
        o_cmp = _dot(vct_ref[...], p.astype(BF16))
        p_sum = p[:, 0:t]
        for hp in range(1, NSA_HPG):
            p_sum = p_sum + p[:, hp * t:(hp + 1) * t]
        ps_hi = p_sum.astype(BF16)
        ps_lo = (p_sum - ps_hi.astype(F32)).astype(BF16)
        imp = _dot(ov_t, ps_hi) + _dot(ov_t, ps_lo)
        qpos = q0 + lax.broadcasted_iota(jnp.int32, (nsl, t), 1)
        back = qpos // NSA_SEL_BLOCK - blk_j
        forced = jnp.logical_or(blk_j == 0, jnp.logical_and(back >= 0, back < NSA_LOCAL_BLOCKS))
        valid = blk_j * NSA_SEL_BLOCK <= qpos
        score = jnp.where(valid, jnp.where(forced, FORCE, imp), NEG)
        n_grp = nsl // SUBLANES
        tiles = [score[a * SUBLANES:(a + 1) * SUBLANES] for a in range(n_grp)]
        sub = lax.broadcasted_iota(jnp.int32, (SUBLANES, t), 0)

        def count_group(mt, ranks):
            ranks = list(ranks)
            for r in range(SUBLANES):
                row = tiles[mt][r:r + 1, :]
                for a in range(n_grp):
                    if a < mt:
                        ahead = jnp.where(row > tiles[a], 1, 0)
                    elif a > mt:
                        ahead = jnp.where(row >= tiles[a], 1, 0)
                    else:
                        ahead = jnp.where(sub > r, jnp.where(row >= tiles[a], 1, 0), jnp.where(row > tiles[a], 1, 0))
                    ranks[a] = ranks[a] + ahead
            return tuple(ranks)

        ranks = tuple(jnp.zeros((SUBLANES, t), jnp.int32) for _ in range(n_grp))
        last_grp = (q0 + t - 1) // (NSA_SEL_BLOCK * SUBLANES)
        for mt in range(n_grp):
            ranks = lax.cond(mt <= last_grp, functools.partial(count_group, mt), lambda r: r, ranks)
        sel = jnp.logical_and(jnp.concatenate(ranks, axis=0) < n_top, valid)
        bias = jnp.where(sel, 0.0, NEG).astype(BF16)

        lower = diff <= 0
        s_x = jnp.where(lower, s_win[0], s_win[2])
        m_w = jnp.maximum(jnp.max(s_x, axis=0, keepdims=True), jnp.max(s_win[1], axis=0, keepdims=True))
        p_x = jnp.exp2(s_x - m_w)
        p_win = [jnp.where(lower, p_x, 0.0), jnp.exp2(s_win[1] - m_w), jnp.where(lower, 0.0, p_x)]
        acc_w = functools.reduce(jnp.add, [
            _dot(jnp.concatenate([vwt_ref[:, pl.ds(k0, t)], ones], axis=0), p.astype(BF16))
            for p, k0 in zip(p_win, (q0, k1, k2))])
        o_win = acc_w[:hd] / acc_w[hd:hd + 1]

        qa = [jnp.concatenate([qh, bias], axis=0) for qh in qs]
        for hp in range(NSA_HPG):
            ss_ref[0, hp] = jnp.where(causal, _dot(ks_ref[pl.ds(q0, t), :], qa[hp]), NEG)

        def score(hp, j):
            return _dot(ks_ref[pl.ds(pl.multiple_of(j * t, t), t), :], qa[hp])

        def value(hp, j):
            return jnp.concatenate([vst_ref[:, pl.ds(pl.multiple_of(j * t, t), t)], ones], axis=0)

        carries = _flash_pipelined(ss_ref, qi, qi, score, value, hd, t)
        for hp in range(NSA_HPG):
            _, acc = carries[hp]
            cols = slice(hp * t, (hp + 1) * t)
            gate = [gt_ref[pl.ds(gate_row0 + NSA_N_BRANCH * hp + br, 1), pl.ds(q0, t)] for br in range(NSA_N_BRANCH)]
            o = gate[0] * o_cmp[:, cols] + gate[1] * (acc[:hd] / acc[hd:hd + 1]) + gate[2] * o_win[:, cols]
            ot_ref[hp * hd:(hp + 1) * hd, pl.ds(q0, t)] = o.astype(ot_ref.dtype)
        return 0

    lax.fori_loop(0, seq // t, q_tile, 0)


def _nsa_attention(main_t, k_cmp, v_cmp_t, gates_t, *, batch, seq, row_ks, row_vs, row_kw, row_vw):
    groups, hd, hpg = NSA_KV_GROUPS, NSA_HEAD_DIM, NSA_HPG
    n = batch * seq
    ncr = seq // NSA_CMP_STRIDE
    nsl = seq // NSA_SEL_BLOCK
    gh = hpg * hd
    kv_rows = lambda row0: pl.BlockSpec((hd, seq), lambda b, g: (row0 // hd + g, b))
    return pl.pallas_call(
        functools.partial(_nsa_kernel, seq=seq),
        grid=(batch, groups),
        in_specs=[
            pl.BlockSpec((gh, seq), lambda b, g: (g, b)),
            pl.BlockSpec((None, ncr, hd), lambda b, g: (b * groups + g, 0, 0)),
            pl.BlockSpec((None, hd, ncr), lambda b, g: (b * groups + g, 0, 0)),
            kv_rows(row_ks), kv_rows(row_vs), kv_rows(row_kw), kv_rows(row_vw),
            pl.BlockSpec((gates_t.shape[0], seq), lambda b, g: (0, b)),
        ],
        out_specs=pl.BlockSpec((gh, seq), lambda b, g: (g, b)),
        out_shape=jax.ShapeDtypeStruct((groups * gh, n), BF16),
        scratch_shapes=[pltpu.VMEM((seq, hd + nsl), BF16), pltpu.VMEM((seq, 2 * hd), BF16),
                        pltpu.VMEM((2, hpg, ATTN_TILE, ATTN_TILE), F32)],
        compiler_params=_params("parallel", "parallel"),
        name="nsa_attention",
    )(main_t, k_cmp, v_cmp_t, main_t, main_t, main_t, main_t, gates_t)


def _nsa_layer(h, g, w_in, j, gate_b, q_gain, kc_gain, ks_gain, kw_gain, pos_k, pos_v, ck_w1, ck_w2, cv_w1, cv_w2, w_out,
               *, batch, seq):
    heads, groups, hd, hpg = NSA_HEADS, NSA_KV_GROUPS, NSA_HEAD_DIM, NSA_HPG
    assert seq % ATTN_TILE == 0 and NSA_WINDOW == 2 * ATTN_TILE and NSA_CMP_BLOCK == 2 * NSA_CMP_STRIDE
    qw, kvw = heads * hd, groups * hd
    r_kc, r_vc, r_ks, r_vs, r_kw, r_vw, r_g = (qw + i * kvw for i in range(7))
    n_gate = NSA_N_BRANCH * heads
    ones = jnp.ones((kvw,), F32)
    col = jnp.concatenate([jnp.tile(q_gain, heads) * (hd ** -0.5 * LOG2E), ones, ones, jnp.tile(ks_gain, groups), ones,
                           jnp.tile(kw_gain, groups), ones, gate_b])
    segs = ((0, qw, "norm"), (r_kc, r_ks, "raw"), (r_ks, r_vs, "norm"), (r_vs, r_kw, "raw"), (r_kw, r_vw, "norm"),
            (r_vw, r_g, "raw"), (r_g, r_g + n_gate, "sigmoid"))
    main_t, gates_t = _proj_t(h, g, w_in, j, col.reshape(-1, 1), segs, head_dim=hd)

    l, hid = NSA_CMP_BLOCK, ck_w1.shape[1]
    zero = jnp.zeros((l, hd, hid), F32)
    w1 = jnp.concatenate([jnp.concatenate([ck_w1.reshape(l, hd, hid), zero], axis=2),
                          jnp.concatenate([zero, cv_w1.reshape(l, hd, hid)], axis=2)], axis=1)
    w1_pairs = w1.reshape(l // 2, 4 * hd, 2 * hid).astype(BF16)
    k_cmp, v_cmp_t = _nsa_compress(main_t, jnp.concatenate([pos_k, pos_v], axis=1), w1_pairs, ck_w2.astype(BF16),
                                   cv_w2.T.astype(BF16), kc_gain, batch=batch, seq=seq, row_kc=r_kc, row_vc=r_vc)
    o_t = _nsa_attention(main_t, k_cmp, v_cmp_t, gates_t, batch=batch, seq=seq,
                         row_ks=r_ks, row_vs=r_vs, row_kw=r_kw, row_vw=r_vw)
    return o_t, w_out, j, "feature_major"


def kernel(x, norm_g, ffn1_wi, ffn1_wo, ffn2_wi, ffn2_wo, moba_w_in, moba_q_gain, moba_k_gain, moba_w_out, lru_w_in, lru_conv_w, lru_conv_b, lru_wa, lru_ba, lru_wx, lru_bx, lru_lam, lru_w_out, nsa_w_in, nsa_gate_b, nsa_q_gain, nsa_kc_gain, nsa_ks_gain, nsa_kw_gain, nsa_pos_k, nsa_pos_v, nsa_ck_w1, nsa_ck_w2, nsa_cv_w1, nsa_cv_w2, nsa_w_out):
    batch, seq, d = x.shape
    n_mixers = 3
    h = x.reshape(batch * seq, d)
    for i in range(norm_g.shape[0]):
        j, kind = divmod(i, n_mixers)
        h = _ffn(h, norm_g[i, 0], ffn1_wi, ffn1_wo, i)
        if kind == 0:
            mixed = _moba_layer(h, norm_g[i, 1], moba_w_in, j, moba_q_gain[j], moba_k_gain[j], moba_w_out,
                                batch=batch, seq=seq)
        elif kind == 1:
            mixed = _lru_layer(h, norm_g[i, 1], lru_w_in, j, lru_conv_w[j], lru_conv_b[j], lru_wa[j], lru_ba[j],
                               lru_wx[j], lru_bx[j], lru_lam[j], lru_w_out, batch=batch, seq=seq)
        else:
            mixed = _nsa_layer(h, norm_g[i, 1], nsa_w_in, j, nsa_gate_b[j], nsa_q_gain[j], nsa_kc_gain[j],
                               nsa_ks_gain[j], nsa_kw_gain[j], nsa_pos_k[j], nsa_pos_v[j], nsa_ck_w1[j], nsa_ck_w2[j],
                               nsa_cv_w1[j], nsa_cv_w2[j], nsa_w_out, batch=batch, seq=seq)
        h = _ffn(h, norm_g[i, 2], ffn2_wi, ffn2_wo, i, mixed=mixed)
    return h.reshape(batch, seq, d)
```
